```python
import jax, jax.numpy as jnp
from jax import lax
import numpy as np


D_MODEL = 1024
BATCH = 8
SEQ = 4096
DEPTH = 2
DEC_BATCH = 128
DEC_SEQ = 1
PAST_LEN = 16384
PAGE_SIZE = 128

N_META = 16
N_A_LAYERS = DEPTH // 2
N_B_LAYERS = DEPTH - N_A_LAYERS
POOL_WINDOWS = (2, 4, 8, 16)
N_POOL_GROUPS = len(POOL_WINDOWS)
POOL_GROUP = D_MODEL // N_POOL_GROUPS
POOL_HIST = max(POOL_WINDOWS) - 1
N_HEADS = 8
QK_NOPE = 128
QK_ROPE = 64
V_HEAD = 128
Q_LORA = 384
KV_LORA = 256
ROPE_THETA = 10000.0
D_FF = -(-8 * D_MODEL // (3 * 256)) * 256
ALPHA = (2 * DEPTH) ** 0.25
BETA = (8 * DEPTH) ** -0.25
Q_BLOCK = 128
LN_EPS = 1e-5
RMS_EPS = 1e-6
SM_SCALE = (QK_NOPE + QK_ROPE) ** -0.5

kernel_name = "yoco_pool_mla_decoder_step"


def _layer_norm(x, g, b):
    xf = x.astype(jnp.float32)
    mu = xf.mean(-1, keepdims=True)
    var = jnp.square(xf - mu).mean(-1, keepdims=True)
    y = (xf - mu) * lax.rsqrt(var + LN_EPS) * g.astype(jnp.float32) + b.astype(jnp.float32)
    return y.astype(x.dtype)


def _rms_norm(x, g):
    xf = x.astype(jnp.float32)
    y = xf * lax.rsqrt(jnp.square(xf).mean(-1, keepdims=True) + RMS_EPS) * g.astype(jnp.float32)
    return y.astype(x.dtype)


def _rope(x, pos):
    half = QK_ROPE // 2
    inv = ROPE_THETA ** (-jnp.arange(half, dtype=jnp.float32) / half)
    ang = pos.astype(jnp.float32)[:, None] * inv[None, :]
    shp = (pos.shape[0],) + (1,) * (x.ndim - 3) + (half,)
    cos, sin = jnp.cos(ang).reshape(shp), jnp.sin(ang).reshape(shp)
    xf = x.astype(jnp.float32)
    x1, x2 = xf[..., :half], xf[..., half:]
    return jnp.concatenate([x1 * cos - x2 * sin, x2 * cos + x1 * sin], axis=-1).astype(x.dtype)


def _pool_mixer(x, hist, w, scale):
    p_len, n_new = hist.shape[1], x.shape[1]
    xs = jnp.concatenate([hist, x], axis=1)
    j = jnp.arange(p_len, p_len + n_new)
    outs = []
    for g, win in enumerate(POOL_WINDOWS):
        xg = xs[..., g * POOL_GROUP:(g + 1) * POOL_GROUP].astype(jnp.float32)
        c = jnp.concatenate([jnp.zeros_like(xg[:, :1]), jnp.cumsum(xg, axis=1)], axis=1)
        lo = jnp.maximum(j - win + 1, 0)
        cnt = (j + 1 - lo).astype(jnp.float32)
        outs.append((c[:, j + 1] - c[:, lo]) / cnt[None, :, None] - xg[:, p_len:])
    d = jnp.stack(outs, axis=2).astype(x.dtype)
    y = jnp.einsum('blgc,gce->blge', d, w).reshape(x.shape)
    return y * scale


def _swiglu(x, w_gate, w_up, w_down):
    return (jax.nn.silu(x @ w_gate) * (x @ w_up)) @ w_down


def _shared_latent(x, pos, wkv_a, kv_norm):
    kv = x @ wkv_a
    ckv = _rms_norm(kv[..., :KV_LORA], kv_norm)
    kpe = _rope(kv[..., KV_LORA:], pos)
    return ckv, kpe


def _mla_query(x, pos, wq_a, q_norm, wq_b, wkv_b):
    bsz, n = x.shape[:2]
    q = (_rms_norm(x @ wq_a, q_norm) @ wq_b).reshape(bsz, n, N_HEADS, QK_NOPE + QK_ROPE)
    q_pe = _rope(q[..., QK_NOPE:], pos)
    q_lat = jnp.einsum('blhn,rhn->blhr', q[..., :QK_NOPE], wkv_b[:, :, :QK_NOPE])
    return q_lat, q_pe


def _latent_attention(q_lat, q_pe, q_pos, ckv_past, kpe_past, ckv_new, kpe_new):
    bsz, nq = q_lat.shape[:2]
    n_past = ckv_past.shape[1]
    k_pos = jnp.arange(n_past + ckv_new.shape[1])
    blk = min(Q_BLOCK, nq)
    nb = -(-nq // blk)
    pad = nb * blk - nq

    def blocks(t):
        t = jnp.pad(t, ((0, 0), (0, pad)) + ((0, 0),) * (t.ndim - 2))
        return jnp.moveaxis(t.reshape((bsz, nb, blk) + t.shape[2:]), 1, 0)

    ql_b, qp_b = blocks(q_lat), blocks(q_pe)
    pos_b = jnp.pad(q_pos, (0, pad), mode='edge').reshape(nb, blk)

    def one(args):
        ql, qp, qpos = args

        def scores(ckv, kpe):
            return jnp.einsum('bqhr,bkr->bhqk', ql, ckv) + jnp.einsum('bqhp,bkp->bhqk', qp, kpe)

        s = jnp.concatenate([scores(ckv_past, kpe_past), scores(ckv_new, kpe_new)], axis=-1)
        s = s.astype(jnp.float32) * SM_SCALE
        s = jnp.where(k_pos[None, :] <= qpos[:, None], s, -jnp.inf)
        p = jax.nn.softmax(s, axis=-1).astype(ckv_new.dtype)
        return (jnp.einsum('bhqk,bkr->bqhr', p[..., :n_past], ckv_past)
                + jnp.einsum('bhqk,bkr->bqhr', p[..., n_past:], ckv_new))

    o = lax.map(one, (ql_b, qp_b, pos_b))
    o = jnp.moveaxis(o, 0, 1).reshape((bsz, nb * blk) + o.shape[3:])
    return o[:, :nq]


def _trunk(x, pos0, pool_hist, ckv_past, kpe_past, ln_g, ln_b, pool_w, pool_scale,
           w_gate, w_up, w_down, wq_a, q_norm, wq_b, wo, wkv_a, kv_norm, wkv_b):
    bsz, n = x.shape[:2]
    pos = pos0 + jnp.arange(n)
    new_pool = []
    ckv_new = kpe_new = None
    for l in range(DEPTH):
        if l < N_A_LAYERS:
            hist = pool_hist[l]
            new_pool.append(jnp.concatenate([hist, x], axis=1)[:, -POOL_HIST:])
            h = _pool_mixer(x, hist, pool_w[l], pool_scale[l])
        else:
            b = l - N_A_LAYERS
            if ckv_new is None:
                ckv_new, kpe_new = _shared_latent(x, pos, wkv_a, kv_norm)
            q_lat, q_pe = _mla_query(x, pos, wq_a[b], q_norm[b], wq_b[b], wkv_b)
            o = _latent_attention(q_lat, q_pe, pos, ckv_past, kpe_past, ckv_new, kpe_new)
            h = jnp.einsum('blhr,rhv->blhv', o, wkv_b[:, :, QK_NOPE:]).reshape(bsz, n, N_HEADS * V_HEAD) @ wo[b]
        x = _layer_norm(ALPHA * x + h, ln_g[l, 0], ln_b[l, 0])
        x = _layer_norm(ALPHA * x + _swiglu(x, w_gate[l], w_up[l], w_down[l]), ln_g[l, 1], ln_b[l, 1])
    return x, jnp.stack(new_pool), ckv_new, kpe_new


def setup_inputs(seed: int = 0) -> dict:
    key = jax.random.key(seed)
    ks = jax.random.split(key, 24)
    n_pages = PAST_LEN // PAGE_SIZE
    n_used = DEC_BATCH * n_pages
    n_pool_pages = n_used + n_used // 4

    def nrm(k, shape, scale):
        return jax.random.normal(k, shape, jnp.float32) * scale

    page_table = jax.random.permutation(ks[5], n_pool_pages)[:n_used].reshape(DEC_BATCH, n_pages).astype(jnp.int32)
    return {
        'x_prompt': nrm(ks[0], (BATCH, SEQ, D_MODEL), 1.0),
        'x_sample': nrm(ks[1], (DEC_BATCH, DEC_SEQ, D_MODEL), 1.0),
        'state_pool': nrm(ks[2], (N_A_LAYERS, DEC_BATCH, POOL_HIST, D_MODEL), 1.0),
        'cache_ckv': nrm(ks[3], (n_pool_pages, PAGE_SIZE, KV_LORA), 1.0),
        'cache_kpe': nrm(ks[4], (n_pool_pages, PAGE_SIZE, QK_ROPE), 1.0),
        'page_table': page_table,
        'meta_tokens': nrm(ks[6], (N_META, D_MODEL), 1.0),
        'ln_g': 1.0 + nrm(ks[7], (DEPTH, 2, D_MODEL), 0.01),
        'ln_b': nrm(ks[8], (DEPTH, 2, D_MODEL), 0.01),
        'pool_w': nrm(ks[9], (N_A_LAYERS, N_POOL_GROUPS, POOL_GROUP, POOL_GROUP), POOL_GROUP ** -0.5 * BETA),
        'pool_scale': 1.0 + nrm(ks[10], (N_A_LAYERS, D_MODEL), 0.1),
        'w_gate': nrm(ks[11], (DEPTH, D_MODEL, D_FF), D_MODEL ** -0.5),
        'w_up': nrm(ks[12], (DEPTH, D_MODEL, D_FF), D_MODEL ** -0.5),
        'w_down': nrm(ks[13], (DEPTH, D_FF, D_MODEL), D_FF ** -0.5 * BETA),
        'wq_a': nrm(ks[14], (N_B_LAYERS, D_MODEL, Q_LORA), D_MODEL ** -0.5),
        'q_norm': 1.0 + nrm(ks[15], (N_B_LAYERS, Q_LORA), 0.01),
        'wq_b': nrm(ks[16], (N_B_LAYERS, Q_LORA, N_HEADS * (QK_NOPE + QK_ROPE)), Q_LORA ** -0.5),
        'wo': nrm(ks[17], (N_B_LAYERS, N_HEADS * V_HEAD, D_MODEL), (N_HEADS * V_HEAD) ** -0.5 * BETA),
        'wkv_a': nrm(ks[18], (D_MODEL, KV_LORA + QK_ROPE), D_MODEL ** -0.5),
        'kv_norm': 1.0 + nrm(ks[19], (KV_LORA,), 0.01),
        'wkv_b': nrm(ks[20], (KV_LORA, N_HEADS, QK_NOPE + V_HEAD), KV_LORA ** -0.5),
    }


def reference(x_prompt, x_sample, state_pool, cache_ckv, cache_kpe, page_table, meta_tokens,
              ln_g, ln_b, pool_w, pool_scale, w_gate, w_up, w_down,
              wq_a, q_norm, wq_b, wo, wkv_a, kv_norm, wkv_b):
    weights = (ln_g, ln_b, pool_w, pool_scale, w_gate, w_up, w_down,
               wq_a, q_norm, wq_b, wo, wkv_a, kv_norm, wkv_b)

    bsz = x_prompt.shape[0]
    meta = jnp.broadcast_to(meta_tokens.astype(x_prompt.dtype)[None], (bsz, N_META, D_MODEL))
    xp = jnp.concatenate([meta, x_prompt], axis=1)
    empty_hist = [jnp.zeros((bsz, 0, D_MODEL), xp.dtype) for _ in range(N_A_LAYERS)]
    yp, pool_prompt, ckv_prompt, kpe_prompt = _trunk(
        xp, 0, empty_hist,
        jnp.zeros((bsz, 0, KV_LORA), xp.dtype), jnp.zeros((bsz, 0, QK_ROPE), xp.dtype),
        *weights)
    y_prompt = yp[:, N_META:]

    dbsz, n_pages = page_table.shape
    past_len = n_pages * PAGE_SIZE
    ckv_past = cache_ckv[page_table].reshape(dbsz, past_len, KV_LORA)
    kpe_past = cache_kpe[page_table].reshape(dbsz, past_len, QK_ROPE)
    hist = [state_pool[i] for i in range(N_A_LAYERS)]
    y_sample, pool_sample, ckv_sample, kpe_sample = _trunk(
        x_sample, past_len, hist, ckv_past, kpe_past, *weights)

    return (y_prompt, y_sample, pool_prompt, pool_sample, ckv_prompt, kpe_prompt, ckv_sample, kpe_sample)
```

```python
import functools

import jax
import jax.numpy as jnp
from jax import lax
from jax.experimental import pallas as pl
from jax.experimental.pallas import tpu as pltpu

F32 = jnp.float32
BF16 = jnp.bfloat16

D_MODEL = 1024
DEPTH = 2
N_META = 16
POOL_WINDOWS = (2, 4, 8, 16)
POOL_GROUP = D_MODEL // len(POOL_WINDOWS)
POOL_HIST = max(POOL_WINDOWS) - 1
N_HEADS = 8
QK_NOPE = 128
QK_ROPE = 64
V_HEAD = 128
KV_LORA = 256
ROPE_THETA = 10000.0
ALPHA = (2 * DEPTH) ** 0.25
LN_EPS = 1e-5
RMS_EPS = 1e-6
SM_SCALE = (QK_NOPE + QK_ROPE) ** -0.5
PAGE_SIZE = 128

LANES = 128
MASK_VALUE = -1e30
VMEM_LIMIT = 48 * 1024 * 1024

ROW_TILE = 512
ATTN_TILE = 256
PAGES_PER_CHUNK = 16


def _dot(a, b):
    return jnp.dot(a, b, preferred_element_type=F32)


def _dot_nt(a, b):
    return lax.dot_general(a, b, (((1,), (1,)), ((), ())), preferred_element_type=F32)


def _layer_norm(x, g, b):
    mu = jnp.mean(x, axis=-1, keepdims=True)
    xc = x - mu
    var = jnp.mean(xc * xc, axis=-1, keepdims=True)
    return xc * lax.rsqrt(var + LN_EPS) * g + b


def _rms_norm(x, g):
    return x * lax.rsqrt(jnp.mean(x * x, axis=-1, keepdims=True) + RMS_EPS) * g


def _const_spec(shape):
    zeros = (0,) * len(shape)
    return pl.BlockSpec(shape, lambda *_: zeros, pipeline_mode=pl.Buffered(1))


def _params(n_axes):
    return pltpu.CompilerParams(dimension_semantics=("arbitrary",) * n_axes,
                                vmem_limit_bytes=VMEM_LIMIT)


def _window_sums(xs, n_rows, pad):
    hi = xs.astype(BF16)
    lo = (xs - hi.astype(F32)).astype(BF16)
    k = xs.shape[0]
    diff = (lax.broadcasted_iota(jnp.int32, (n_rows, k), 0) + pad
            - lax.broadcasted_iota(jnp.int32, (n_rows, k), 1))
    sums = []
    for g, win in enumerate(POOL_WINDOWS):
        band = jnp.where(diff >= 0, jnp.where(diff < win, 1.0, 0.0), 0.0).astype(BF16)
        cols = slice(g * POOL_GROUP, (g + 1) * POOL_GROUP)
        sums.append(_dot(band, hi[:, cols]) + _dot(band, lo[:, cols]))
    return sums


def _pool_finish(x, means, w_ref, scale, g, b):
    outs = []
    for gi in range(len(POOL_WINDOWS)):
        cols = slice(gi * POOL_GROUP, (gi + 1) * POOL_GROUP)
        d = means[gi] - x[:, cols]
        outs.append(_dot(d.astype(BF16), w_ref[gi]))
    h = jnp.concatenate(outs, axis=1) * scale
    return _layer_norm(ALPHA * x + h, g, b)


def _pool_main_kernel(x_ref, prev_ref, meta_ref, w_ref, scale_ref, g_ref, b_ref, o_ref):
    x = x_ref[0]
    halo = jnp.where(pl.program_id(1) == 0, meta_ref[...], prev_ref[0])
    xs = jnp.concatenate([jnp.zeros((LANES - N_META, D_MODEL), F32), halo, x], axis=0)
    sums = _window_sums(xs, x.shape[0], LANES)
    means = [s * (1.0 / w) for s, w in zip(sums, POOL_WINDOWS)]
    o_ref[0] = _pool_finish(x, means, w_ref, scale_ref[...], g_ref[...], b_ref[...])


def _pool_meta_kernel(x_ref, w_ref, scale_ref, g_ref, b_ref, o_ref):
    x = x_ref[...]
    n = x.shape[0]
    xs = jnp.concatenate([jnp.zeros((LANES - n, D_MODEL), F32), x], axis=0)
    sums = _window_sums(xs, n, LANES - n)
    row = lax.broadcasted_iota(jnp.int32, (n, 1), 0)
    means = [s / jnp.minimum(row + 1, w).astype(F32) for s, w in zip(sums, POOL_WINDOWS)]
    o_ref[...] = _pool_finish(x, means, w_ref, scale_ref[...], g_ref[...], b_ref[...])


def _pool_sample_kernel(xs_ref, w_ref, scale_ref, g_ref, b_ref, o_ref):
    n_ctx = xs_ref.shape[0]
    x = xs_ref[n_ctx - 1]
    means = []
    for gi, win in enumerate(POOL_WINDOWS):
        cols = slice(gi * POOL_GROUP, (gi + 1) * POOL_GROUP)
        s = xs_ref[n_ctx - win, :, cols]
        for k in range(n_ctx - win + 1, n_ctx):
            s = s + xs_ref[k, :, cols]
        means.append(s * (1.0 / win))
    o_ref[...] = _pool_finish(x, means, w_ref, scale_ref[...], g_ref[...], b_ref[...])


def _pool_weight_specs():
    return [_const_spec((len(POOL_WINDOWS), POOL_GROUP, POOL_GROUP)),
            _const_spec((1, D_MODEL)), _const_spec((1, D_MODEL)), _const_spec((1, D_MODEL))]


def _pool_main(x, meta, pool_w, scale, g, b):
    bsz, seq, _ = x.shape
    tile = ROW_TILE
    halo_blocks = tile // N_META
    return pl.pallas_call(
        _pool_main_kernel,
        grid=(bsz, seq // tile),
        in_specs=[pl.BlockSpec((1, tile, D_MODEL), lambda bi, t: (bi, t, 0)),
                  pl.BlockSpec((1, N_META, D_MODEL),
                               lambda bi, t: (bi, jnp.maximum(t * halo_blocks - 1, 0), 0)),
                  _const_spec((N_META, D_MODEL))] + _pool_weight_specs(),
        out_specs=pl.BlockSpec((1, tile, D_MODEL), lambda bi, t: (bi, t, 0)),
        out_shape=jax.ShapeDtypeStruct(x.shape, F32),
        compiler_params=_params(2),
        name="pool_main",
    )(x, x, meta, pool_w, scale, g, b)


def _pool_meta(meta, pool_w, scale, g, b):
    return pl.pallas_call(
        _pool_meta_kernel,
        grid=(1,),
        in_specs=[_const_spec(meta.shape)] + _pool_weight_specs(),
        out_specs=_const_spec(meta.shape),
        out_shape=jax.ShapeDtypeStruct(meta.shape, F32),
        compiler_params=_params(1),
        name="pool_meta",
    )(meta, pool_w, scale, g, b)


def _pool_sample(xs, pool_w, scale, g, b):
    rows = xs.shape[1]
    return pl.pallas_call(
        _pool_sample_kernel,
        grid=(1,),
        in_specs=[_const_spec(xs.shape)] + _pool_weight_specs(),
        out_specs=_const_spec((rows, D_MODEL)),
        out_shape=jax.ShapeDtypeStruct((rows, D_MODEL), F32),
        compiler_params=_params(1),
        name="pool_sample",
    )(xs, pool_w, scale, g, b)


def _ffn_kernel(*refs, with_proj, n_chunks):
    if with_proj:
        a_ref, x_ref, wo_ref, g1_ref, b1_ref = refs[:5]
        refs = refs[5:]
        x = _layer_norm(ALPHA * x_ref[...] + _dot(a_ref[...], wo_ref[...]), g1_ref[...], b1_ref[...])
    else:
        x_ref = refs[0]
        refs = refs[1:]
        x = x_ref[...]
    wg_ref, wu_ref, wd_ref, g2_ref, b2_ref, o_ref = refs
    xb = x.astype(BF16)
    chunk = wg_ref.shape[1] // n_chunks
    acc = None
    for c in range(n_chunks):
        cols = slice(c * chunk, (c + 1) * chunk)
        gate = _dot(xb, wg_ref[:, cols])
        up = _dot(xb, wu_ref[:, cols])
        mid = (gate * (1.0 / (1.0 + jnp.exp(-gate))) * up).astype(BF16)
        part = _dot(mid, wd_ref[cols, :])
        acc = part if acc is None else acc + part
    o_ref[...] = _layer_norm(ALPHA * x + acc, g2_ref[...], b2_ref[...])


def _ffn(x, wg, wu, wd, g2, b2, proj=None):
    rows = x.shape[0]
    tile = min(ROW_TILE, rows)
    d_ff = wg.shape[1]
    row_spec = pl.BlockSpec((tile, D_MODEL), lambda i: (i, 0))
    vec_spec = _const_spec((1, D_MODEL))
    args, specs = [], []
    if proj is not None:
        a, wo, g1, b1 = proj
        args += [a, x, wo, g1, b1]
        specs += [row_spec, row_spec, _const_spec(wo.shape), vec_spec, vec_spec]
    else:
        args += [x]
        specs += [row_spec]
    args += [wg, wu, wd, g2, b2]
    specs += [_const_spec(wg.shape), _const_spec(wu.shape), _const_spec(wd.shape), vec_spec, vec_spec]
    return pl.pallas_call(
        functools.partial(_ffn_kernel, with_proj=proj is not None, n_chunks=2),
        grid=(rows // tile,),
        in_specs=specs,
        out_specs=row_spec,
        out_shape=jax.ShapeDtypeStruct((rows, D_MODEL), F32),
        compiler_params=_params(1),
        name="ffn_proj" if proj is not None else "ffn",
    )(*args)


def _latent_query_kernel(x_ref, cos_ref, sin_ref, wkva_ref, kvn_ref, wqa_ref, qn_ref, wqb_ref, wk_ref,
                         ckv_ref, kpe_ref, klat_ref, kpeb_ref, qlat_ref, qpe_ref):
    xb = x_ref[...].astype(BF16)
    cos = cos_ref[...]
    sin = sin_ref[...]
    kv = _dot(xb, wkva_ref[...])
    ckv = _rms_norm(kv[:, :KV_LORA], kvn_ref[...])
    kpe = kv[:, KV_LORA:KV_LORA + QK_ROPE] * cos + kv[:, KV_LORA + QK_ROPE:] * sin
    ckv_ref[...] = ckv
    kpe_ref[...] = kpe
    klat_ref[...] = ckv.astype(BF16)
    kpeb_ref[...] = kpe.astype(BF16)
    qa = _rms_norm(_dot(xb, wqa_ref[...]), qn_ref[...]).astype(BF16)
    q = _dot(qa, wqb_ref[...])
    per_head = QK_NOPE + 2 * QK_ROPE
    for h in range(N_HEADS):
        qh = q[:, h * per_head:(h + 1) * per_head]
        qlat = _dot(qh[:, :QK_NOPE].astype(BF16), wk_ref[h]) * SM_SCALE
        qpe = (qh[:, QK_NOPE:QK_NOPE + QK_ROPE] * cos + qh[:, QK_NOPE + QK_ROPE:] * sin) * SM_SCALE
        qlat_ref[h] = qlat.astype(BF16)
        qpe_ref[h] = qpe.astype(BF16)


def _latent_query(x, cos, sin, wkva, kvn, wqa, qn, wqb, wk):
    rows = x.shape[0]
    tile = min(ROW_TILE, rows)
    pos_tiles = cos.shape[0] // tile
    row = lambda width: pl.BlockSpec((tile, width), lambda i: (i, 0))
    pos = pl.BlockSpec((tile, QK_ROPE), lambda i: (i % pos_tiles, 0))
    head = lambda width: pl.BlockSpec((N_HEADS, tile, width), lambda i: (0, i, 0))
    return pl.pallas_call(
        _latent_query_kernel,
        grid=(rows // tile,),
        in_specs=[row(D_MODEL), pos, pos, _const_spec(wkva.shape), _const_spec(kvn.shape),
                  _const_spec(wqa.shape), _const_spec(qn.shape), _const_spec(wqb.shape),
                  _const_spec(wk.shape)],
        out_specs=[row(KV_LORA), row(QK_ROPE), row(KV_LORA), row(QK_ROPE), head(KV_LORA), head(QK_ROPE)],
        out_shape=[jax.ShapeDtypeStruct((rows, KV_LORA), F32),
                   jax.ShapeDtypeStruct((rows, QK_ROPE), F32),
                   jax.ShapeDtypeStruct((rows, KV_LORA), BF16),
                   jax.ShapeDtypeStruct((rows, QK_ROPE), BF16),
                   jax.ShapeDtypeStruct((N_HEADS, rows, KV_LORA), BF16),
                   jax.ShapeDtypeStruct((N_HEADS, rows, QK_ROPE), BF16)],
        compiler_params=_params(1),
        name="latent_query",
    )(x, cos, sin, wkva, kvn, wqa, qn, wqb, wk)


def _flash_kernel(ql_ref, qp_ref, kl_ref, kp_ref, ml_ref, mp_ref, wv_ref, o_ref, m_sc, l_sc, acc_sc):
    tq = ql_ref.shape[1]
    rows = N_HEADS * tq
    i = pl.program_id(1)
    ql = ql_ref[...].reshape(rows, KV_LORA)
    qp = qp_ref[...].reshape(rows, QK_ROPE)

    ml = ml_ref[...]
    s0 = _dot_nt(ql, ml) + _dot_nt(qp, mp_ref[...])
    col = lax.broadcasted_iota(jnp.int32, s0.shape, 1)
    s0 = jnp.where(col < N_META, s0, MASK_VALUE)
    m0 = jnp.max(s0, axis=1, keepdims=True)
    p0 = jnp.exp(s0 - m0)
    m_sc[...] = m0
    l_sc[...] = jnp.sum(p0, axis=1, keepdims=True)
    acc_sc[...] = _dot(p0.astype(BF16), ml)

    def step(kl, kp, diagonal):
        s = _dot_nt(ql, kl) + _dot_nt(qp, kp)
        if diagonal:
            q_row = lax.broadcasted_iota(jnp.int32, s.shape, 0) & (tq - 1)
            k_row = lax.broadcasted_iota(jnp.int32, s.shape, 1)
            s = jnp.where(k_row <= q_row, s, MASK_VALUE)
        m_prev = m_sc[...]
        m_new = jnp.maximum(m_prev, jnp.max(s, axis=1, keepdims=True))
        alpha = jnp.exp(m_prev - m_new)
        p = jnp.exp(s - m_new)
        l_sc[...] = alpha * l_sc[...] + jnp.sum(p, axis=1, keepdims=True)
        acc_sc[...] = alpha * acc_sc[...] + _dot(p.astype(BF16), kl)
        m_sc[...] = m_new

    def body(j, carry):
        off = pl.multiple_of(j * tq, tq)
        step(kl_ref[pl.ds(off, tq), :], kp_ref[pl.ds(off, tq), :], False)
        return carry

    lax.fori_loop(0, i, body, 0)
    off = pl.multiple_of(i * tq, tq)
    step(kl_ref[pl.ds(off, tq), :], kp_ref[pl.ds(off, tq), :], True)

    o = (acc_sc[...] / l_sc[...]).astype(BF16)
    for h in range(N_HEADS):
        o_ref[:, h * V_HEAD:(h + 1) * V_HEAD] = _dot(o[h * tq:(h + 1) * tq], wv_ref[h]).astype(BF16)


def _flash(qlat, qpe, klat, kpe, mlat, mpe, wv, bsz):
    rows = klat.shape[0]
    seq = rows // bsz
    tq = ATTN_TILE
    assert tq & (tq - 1) == 0 and seq % tq == 0
    nq = seq // tq
    return pl.pallas_call(
        _flash_kernel,
        grid=(bsz, nq),
        in_specs=[pl.BlockSpec((N_HEADS, tq, KV_LORA), lambda b, i: (0, b * nq + i, 0)),
                  pl.BlockSpec((N_HEADS, tq, QK_ROPE), lambda b, i: (0, b * nq + i, 0)),
                  pl.BlockSpec((seq, KV_LORA), lambda b, i: (b, 0)),
                  pl.BlockSpec((seq, QK_ROPE), lambda b, i: (b, 0)),
                  _const_spec(mlat.shape), _const_spec(mpe.shape), _const_spec(wv.shape)],
        out_specs=pl.BlockSpec((tq, N_HEADS * V_HEAD), lambda b, i: (b * nq + i, 0)),
        out_shape=jax.ShapeDtypeStruct((rows, N_HEADS * V_HEAD), BF16),
        scratch_shapes=[pltpu.VMEM((N_HEADS * tq, 1), F32), pltpu.VMEM((N_HEADS * tq, 1), F32),
                        pltpu.VMEM((N_HEADS * tq, KV_LORA), F32)],
        compiler_params=_params(2),
        name="flash_prompt",
    )(qlat, qpe, klat, kpe, mlat, mpe, wv)


def _decode_kernel(pt_ref, ql_ref, qp_ref, kn_ref, pn_ref, ckv_hbm, kpe_hbm, o_ref, kbuf, pbuf, sem,
                   *, n_chunks):
    b = pl.program_id(0)

    def chunk_copies(c, slot):
        copies = []
        for p in range(PAGES_PER_CHUNK):
            page = pt_ref[b, c * PAGES_PER_CHUNK + p]
            dst = pl.ds(p * PAGE_SIZE, PAGE_SIZE)
            copies.append(pltpu.make_async_copy(ckv_hbm.at[page], kbuf.at[slot, dst, :], sem.at[0, slot]))
            copies.append(pltpu.make_async_copy(kpe_hbm.at[page], pbuf.at[slot, dst, :], sem.at[1, slot]))
        return copies

    for cp in chunk_copies(0, 0):
        cp.start()

    ql = ql_ref[0]
    qp = qp_ref[0]
    kn = kn_ref[0].astype(F32)
    pn = pn_ref[0].astype(F32)
    m0 = (jnp.sum(ql.astype(F32) * kn, axis=1, keepdims=True)
          + jnp.sum(qp.astype(F32) * pn, axis=1, keepdims=True))
    l0 = jnp.ones_like(m0)
    acc0 = jnp.broadcast_to(kn, (N_HEADS, KV_LORA))

    def body(c, carry):
        m_prev, l_prev, acc = carry
        slot = c % 2

        @pl.when(c + 1 < n_chunks)
        def _():
            for cp in chunk_copies(c + 1, 1 - slot):
                cp.start()

        for cp in chunk_copies(c, slot):
            cp.wait()
        k = kbuf[slot].astype(BF16)
        pe = pbuf[slot].astype(BF16)
        s = _dot_nt(ql, k) + _dot_nt(qp, pe)
        m_new = jnp.maximum(m_prev, jnp.max(s, axis=1, keepdims=True))
        alpha = jnp.exp(m_prev - m_new)
        p = jnp.exp(s - m_new)
        l_new = alpha * l_prev + jnp.sum(p, axis=1, keepdims=True)
        acc_new = alpha * acc + _dot(p.astype(BF16), k)
        return m_new, l_new, acc_new

    _, l_fin, acc_fin = lax.fori_loop(0, n_chunks, body, (m0, l0, acc0))
    o_ref[0] = acc_fin / l_fin


def _decode_attention(page_table, qlat, qpe, knew, pnew, cache_ckv, cache_kpe):
    n_samples, n_pages = page_table.shape
    chunk_rows = PAGES_PER_CHUNK * PAGE_SIZE
    per_sample = lambda width, lead: pl.BlockSpec((1, lead, width), lambda b, pt: (b, 0, 0))
    grid_spec = pltpu.PrefetchScalarGridSpec(
        num_scalar_prefetch=1,
        grid=(n_samples,),
        in_specs=[per_sample(KV_LORA, N_HEADS), per_sample(QK_ROPE, N_HEADS),
                  per_sample(KV_LORA, 1), per_sample(QK_ROPE, 1),
                  pl.BlockSpec(memory_space=pl.ANY), pl.BlockSpec(memory_space=pl.ANY)],
        out_specs=per_sample(KV_LORA, N_HEADS),
        scratch_shapes=[pltpu.VMEM((2, chunk_rows, KV_LORA), F32),
                        pltpu.VMEM((2, chunk_rows, QK_ROPE), F32),
                        pltpu.SemaphoreType.DMA((2, 2))],
    )
    return pl.pallas_call(
        functools.partial(_decode_kernel, n_chunks=n_pages // PAGES_PER_CHUNK),
        grid_spec=grid_spec,
        out_shape=jax.ShapeDtypeStruct((n_samples, N_HEADS, KV_LORA), F32),
        compiler_params=_params(1),
        name="decode_attention",
    )(page_table, qlat, qpe, knew, pnew, cache_ckv, cache_kpe)


def _value_proj_kernel(o_ref, wv_ref, a_ref):
    for h in range(N_HEADS):
        a_ref[:, h * V_HEAD:(h + 1) * V_HEAD] = _dot(o_ref[h].astype(BF16), wv_ref[h]).astype(BF16)


def _value_proj(o, wv):
    rows = o.shape[1]
    return pl.pallas_call(
        _value_proj_kernel,
        grid=(1,),
        in_specs=[_const_spec(o.shape), _const_spec(wv.shape)],
        out_specs=_const_spec((rows, N_HEADS * V_HEAD)),
        out_shape=jax.ShapeDtypeStruct((rows, N_HEADS * V_HEAD), BF16),
        compiler_params=_params(1),
        name="value_proj",
    )(o, wv)


def _rope_tables(pos):
    half = QK_ROPE // 2
    inv = ROPE_THETA ** (-jnp.arange(half, dtype=F32) / half)
    ang = pos.astype(F32)[:, None] * inv[None, :]
    cos, sin = jnp.cos(ang), jnp.sin(ang)
    return jnp.concatenate([cos, cos], axis=1), jnp.concatenate([-sin, sin], axis=1)


def _rotate_half_cols(w):
    half = QK_ROPE // 2
    return jnp.concatenate([w[..., half:], w[..., :half]], axis=-1)


def kernel(x_prompt, x_sample, state_pool, cache_ckv, cache_kpe, page_table, meta_tokens,
           ln_g, ln_b, pool_w, pool_scale, w_gate, w_up, w_down,
           wq_a, q_norm, wq_b, wo, wkv_a, kv_norm, wkv_b):
    bsz, seq, _ = x_prompt.shape
    n_samples = x_sample.shape[0]
    past_len = page_table.shape[1] * PAGE_SIZE

    vec = lambda v: v.reshape(1, -1).astype(F32)
    pool_wb = pool_w[0].astype(BF16)
    wg, wu, wd = w_gate.astype(BF16), w_up.astype(BF16), w_down.astype(BF16)
    wkva = jnp.concatenate([wkv_a, _rotate_half_cols(wkv_a[:, KV_LORA:])], axis=1).astype(BF16)
    wqb3 = wq_b[0].reshape(wq_b.shape[1], N_HEADS, QK_NOPE + QK_ROPE)
    wqb = jnp.concatenate([wqb3, _rotate_half_cols(wqb3[..., QK_NOPE:])], axis=-1)
    wqb = wqb.reshape(wq_b.shape[1], -1).astype(BF16)
    wk = jnp.transpose(wkv_b[:, :, :QK_NOPE], (1, 2, 0)).astype(BF16)
    wv = jnp.transpose(wkv_b[:, :, QK_NOPE:], (1, 0, 2)).astype(BF16)
    wqa, wob = wq_a[0].astype(BF16), wo[0].astype(BF16)
    ln = lambda l, k: (vec(ln_g[l, k]), vec(ln_b[l, k]))

    meta = meta_tokens.astype(F32)
    pool_args = (pool_wb, vec(pool_scale[0])) + ln(0, 0)
    x1_main = _pool_main(x_prompt, meta, *pool_args).reshape(bsz * seq, D_MODEL)
    x1_meta = _pool_meta(meta, *pool_args)
    xs_sample = jnp.concatenate([jnp.swapaxes(state_pool[0], 0, 1), jnp.swapaxes(x_sample, 0, 1)], axis=0)
    x1_sample = _pool_sample(xs_sample, *pool_args)
    x1_small = jnp.concatenate([x1_sample, x1_meta], axis=0)

    ffn0 = (wg[0], wu[0], wd[0]) + ln(0, 1)
    x2_main = _ffn(x1_main, *ffn0)
    x2_small = _ffn(x1_small, *ffn0)

    cos_main, sin_main = _rope_tables(N_META + jnp.arange(seq))
    pos_small = jnp.concatenate([jnp.full((n_samples,), past_len), jnp.arange(N_META)])
    cos_small, sin_small = _rope_tables(pos_small)
    lq_w = (wkva, vec(kv_norm), wqa, vec(q_norm[0]), wqb, wk)
    ckv_main, kpe_main, klat_main, kpeb_main, qlat_main, qpe_main = _latent_query(
        x2_main, cos_main, sin_main, *lq_w)
    ckv_small, kpe_small, klat_small, kpeb_small, qlat_small, qpe_small = _latent_query(
        x2_small, cos_small, sin_small, *lq_w)

    pad_keys = lambda k: jnp.pad(k[n_samples:], ((0, LANES - N_META), (0, 0)))
    a_main = _flash(qlat_main, qpe_main, klat_main, kpeb_main,
                    pad_keys(klat_small), pad_keys(kpeb_small), wv, bsz)
    o_sample = _decode_attention(
        page_table,
        jnp.swapaxes(qlat_small[:, :n_samples], 0, 1), jnp.swapaxes(qpe_small[:, :n_samples], 0, 1),
        klat_small[:n_samples, None, :], kpeb_small[:n_samples, None, :],
        cache_ckv, cache_kpe)
    a_sample = _value_proj(jnp.swapaxes(o_sample, 0, 1), wv)

    ffn1 = (wg[1], wu[1], wd[1]) + ln(1, 1)
    y_main = _ffn(x2_main, *ffn1, proj=(a_main, wob) + ln(1, 0))
    y_sample = _ffn(x2_small[:n_samples], *ffn1, proj=(a_sample, wob) + ln(1, 0))

    pool_prompt = x_prompt[None, :, seq - POOL_HIST:, :]
    pool_sample = jnp.concatenate([state_pool[0][:, 1:], x_sample], axis=1)[None]
    bcast = lambda t: jnp.broadcast_to(t[n_samples:][None], (bsz, N_META, t.shape[1]))
    ckv_prompt = jnp.concatenate([bcast(ckv_small), ckv_main.reshape(bsz, seq, KV_LORA)], axis=1)
    kpe_prompt = jnp.concatenate([bcast(kpe_small), kpe_main.reshape(bsz, seq, QK_ROPE)], axis=1)
    return (y_main.reshape(bsz, seq, D_MODEL), y_sample.reshape(n_samples, 1, D_MODEL),
            pool_prompt, pool_sample, ckv_prompt, kpe_prompt,
            ckv_small[:n_samples, None, :], kpe_small[:n_samples, None, :])
```

```python
import functools

import jax
import jax.numpy as jnp
from jax import lax
from jax.experimental import pallas as pl
from jax.experimental.pallas import tpu as pltpu

F32 = jnp.float32
BF16 = jnp.bfloat16

D_MODEL = 1024
DEPTH = 2
N_META = 16
POOL_WINDOWS = (2, 4, 8, 16)
POOL_GROUP = D_MODEL // len(POOL_WINDOWS)
POOL_HIST = max(POOL_WINDOWS) - 1
N_HEADS = 8
QK_NOPE = 128
QK_ROPE = 64
V_HEAD = 128
KV_LORA = 256
ROPE_THETA = 10000.0
ALPHA = (2 * DEPTH) ** 0.25
LN_EPS = 1e-5
RMS_EPS = 1e-6
SM_SCALE = (QK_NOPE + QK_ROPE) ** -0.5
PAGE_SIZE = 128

LANES = 128
MASK_VALUE = -1e30
VMEM_LIMIT = 48 * 1024 * 1024

ROW_TILE = 512
ATTN_TILE = 256
FLASH_KEY_TILE = 512
FLASH_UNIT_ROWS = 256
FLASH_SOFTMAX_LAG = 1
FLASH_VALUE_LAG = 2
LOG2_E = 1.4426950408889634
PAGES_PER_CHUNK = 32


def _dot(a, b):
    return jnp.dot(a, b, preferred_element_type=F32)


def _dot_nt(a, b):
    return lax.dot_general(a, b, (((1,), (1,)), ((), ())), preferred_element_type=F32)


def _layer_norm(x, g, b):
    mu = jnp.mean(x, axis=-1, keepdims=True)
    xc = x - mu
    var = jnp.mean(xc * xc, axis=-1, keepdims=True)
    return xc * lax.rsqrt(var + LN_EPS) * g + b


def _rms_norm(x, g):
    return x * lax.rsqrt(jnp.mean(x * x, axis=-1, keepdims=True) + RMS_EPS) * g


def _const_spec(shape):
    zeros = (0,) * len(shape)
    return pl.BlockSpec(shape, lambda *_: zeros, pipeline_mode=pl.Buffered(1))


def _params(n_axes):
    return pltpu.CompilerParams(dimension_semantics=("arbitrary",) * n_axes,
                                vmem_limit_bytes=VMEM_LIMIT)


def _window_sums(xs, n_rows, pad):
    hi = xs.astype(BF16)
    lo = (xs - hi.astype(F32)).astype(BF16)
    k = xs.shape[0]
    diff = (lax.broadcasted_iota(jnp.int32, (n_rows, k), 0) + pad
            - lax.broadcasted_iota(jnp.int32, (n_rows, k), 1))
    sums = []
    for g, win in enumerate(POOL_WINDOWS):
        band = jnp.where(diff >= 0, jnp.where(diff < win, 1.0, 0.0), 0.0).astype(BF16)
        cols = slice(g * POOL_GROUP, (g + 1) * POOL_GROUP)
        sums.append(_dot(band, hi[:, cols]) + _dot(band, lo[:, cols]))
    return sums


def _pool_finish(x, means, w_ref, scale, g, b):
    outs = []
    for gi in range(len(POOL_WINDOWS)):
        cols = slice(gi * POOL_GROUP, (gi + 1) * POOL_GROUP)
        d = means[gi] - x[:, cols]
        outs.append(_dot(d.astype(BF16), w_ref[gi]))
    h = jnp.concatenate(outs, axis=1) * scale
    return _layer_norm(ALPHA * x + h, g, b)


def _pool_main_kernel(x_ref, prev_ref, meta_ref, w_ref, scale_ref, g_ref, b_ref, o_ref):
    x = x_ref[0]
    halo = jnp.where(pl.program_id(1) == 0, meta_ref[...], prev_ref[0])
    xs = jnp.concatenate([jnp.zeros((LANES - N_META, D_MODEL), F32), halo, x], axis=0)
    sums = _window_sums(xs, x.shape[0], LANES)
    means = [s * (1.0 / w) for s, w in zip(sums, POOL_WINDOWS)]
    o_ref[0] = _pool_finish(x, means, w_ref, scale_ref[...], g_ref[...], b_ref[...])


def _pool_meta_kernel(x_ref, w_ref, scale_ref, g_ref, b_ref, o_ref):
    x = x_ref[...]
    n = x.shape[0]
    xs = jnp.concatenate([jnp.zeros((LANES - n, D_MODEL), F32), x], axis=0)
    sums = _window_sums(xs, n, LANES - n)
    row = lax.broadcasted_iota(jnp.int32, (n, 1), 0)
    means = [s / jnp.minimum(row + 1, w).astype(F32) for s, w in zip(sums, POOL_WINDOWS)]
    o_ref[...] = _pool_finish(x, means, w_ref, scale_ref[...], g_ref[...], b_ref[...])


def _pool_sample_kernel(xs_ref, w_ref, scale_ref, g_ref, b_ref, o_ref):
    n_ctx = xs_ref.shape[0]
    x = xs_ref[n_ctx - 1]
    means = []
    for gi, win in enumerate(POOL_WINDOWS):
        cols = slice(gi * POOL_GROUP, (gi + 1) * POOL_GROUP)
        s = xs_ref[n_ctx - win, :, cols]
        for k in range(n_ctx - win + 1, n_ctx):
            s = s + xs_ref[k, :, cols]
        means.append(s * (1.0 / win))
    o_ref[...] = _pool_finish(x, means, w_ref, scale_ref[...], g_ref[...], b_ref[...])


def _pool_weight_specs():
    return [_const_spec((len(POOL_WINDOWS), POOL_GROUP, POOL_GROUP)),
            _const_spec((1, D_MODEL)), _const_spec((1, D_MODEL)), _const_spec((1, D_MODEL))]


def _pool_main(x, meta, pool_w, scale, g, b):
    bsz, seq, _ = x.shape
    tile = ROW_TILE
    halo_blocks = tile // N_META
    return pl.pallas_call(
        _pool_main_kernel,
        grid=(bsz, seq // tile),
        in_specs=[pl.BlockSpec((1, tile, D_MODEL), lambda bi, t: (bi, t, 0)),
                  pl.BlockSpec((1, N_META, D_MODEL),
                               lambda bi, t: (bi, jnp.maximum(t * halo_blocks - 1, 0), 0)),
                  _const_spec((N_META, D_MODEL))] + _pool_weight_specs(),
        out_specs=pl.BlockSpec((1, tile, D_MODEL), lambda bi, t: (bi, t, 0)),
        out_shape=jax.ShapeDtypeStruct(x.shape, F32),
        compiler_params=_params(2),
        name="pool_main",
    )(x, x, meta, pool_w, scale, g, b)


def _pool_meta(meta, pool_w, scale, g, b):
    return pl.pallas_call(
        _pool_meta_kernel,
        grid=(1,),
        in_specs=[_const_spec(meta.shape)] + _pool_weight_specs(),
        out_specs=_const_spec(meta.shape),
        out_shape=jax.ShapeDtypeStruct(meta.shape, F32),
        compiler_params=_params(1),
        name="pool_meta",
    )(meta, pool_w, scale, g, b)


def _pool_sample(xs, pool_w, scale, g, b):
    rows = xs.shape[1]
    return pl.pallas_call(
        _pool_sample_kernel,
        grid=(1,),
        in_specs=[_const_spec(xs.shape)] + _pool_weight_specs(),
        out_specs=_const_spec((rows, D_MODEL)),
        out_shape=jax.ShapeDtypeStruct((rows, D_MODEL), F32),
        compiler_params=_params(1),
        name="pool_sample",
    )(xs, pool_w, scale, g, b)


def _ffn_kernel(*refs, with_proj, n_chunks):
    if with_proj:
        a_ref, x_ref, wo_ref, g1_ref, b1_ref = refs[:5]
        refs = refs[5:]
        x = _layer_norm(ALPHA * x_ref[...] + _dot(a_ref[...], wo_ref[...]), g1_ref[...], b1_ref[...])
    else:
        x_ref = refs[0]
        refs = refs[1:]
        x = x_ref[...]
    wg_ref, wu_ref, wd_ref, g2_ref, b2_ref, o_ref = refs
    xb = x.astype(BF16)
    chunk = wg_ref.shape[1] // n_chunks
    acc = None
    for c in range(n_chunks):
        cols = slice(c * chunk, (c + 1) * chunk)
        gate = _dot(xb, wg_ref[:, cols])
        up = _dot(xb, wu_ref[:, cols])
        mid = (gate * (1.0 / (1.0 + jnp.exp(-gate))) * up).astype(BF16)
        part = _dot(mid, wd_ref[cols, :])
        acc = part if acc is None else acc + part
    o_ref[...] = _layer_norm(ALPHA * x + acc, g2_ref[...], b2_ref[...])


def _ffn(x, wg, wu, wd, g2, b2, proj=None):
    rows = x.shape[0]
    tile = min(ROW_TILE, rows)
    d_ff = wg.shape[1]
    row_spec = pl.BlockSpec((tile, D_MODEL), lambda i: (i, 0))
    vec_spec = _const_spec((1, D_MODEL))
    args, specs = [], []
    if proj is not None:
        a, wo, g1, b1 = proj
        args += [a, x, wo, g1, b1]
        specs += [row_spec, row_spec, _const_spec(wo.shape), vec_spec, vec_spec]
    else:
        args += [x]
        specs += [row_spec]
    args += [wg, wu, wd, g2, b2]
    specs += [_const_spec(wg.shape), _const_spec(wu.shape), _const_spec(wd.shape), vec_spec, vec_spec]
    return pl.pallas_call(
        functools.partial(_ffn_kernel, with_proj=proj is not None, n_chunks=2),
        grid=(rows // tile,),
        in_specs=specs,
        out_specs=row_spec,
        out_shape=jax.ShapeDtypeStruct((rows, D_MODEL), F32),
        compiler_params=_params(1),
        name="ffn_proj" if proj is not None else "ffn",
    )(*args)


def _latent_query_kernel(x_ref, cos_ref, sin_ref, wkva_ref, kvn_ref, wqa_ref, qn_ref, wqb_ref, wk_ref,
                         ckv_ref, kpe_ref, klat_ref, kpeb_ref, qlat_ref, qpe_ref):
    xb = x_ref[...].astype(BF16)
    cos = cos_ref[...]
    sin = sin_ref[...]
    kv = _dot(xb, wkva_ref[...])
    ckv = _rms_norm(kv[:, :KV_LORA], kvn_ref[...])
    kpe = kv[:, KV_LORA:KV_LORA + QK_ROPE] * cos + kv[:, KV_LORA + QK_ROPE:] * sin
    ckv_ref[...] = ckv
    kpe_ref[...] = kpe
    klat_ref[...] = ckv.astype(BF16)
    kpeb_ref[...] = kpe.astype(BF16)
    qa = _rms_norm(_dot(xb, wqa_ref[...]), qn_ref[...]).astype(BF16)
    q = _dot(qa, wqb_ref[...])
    per_head = QK_NOPE + 2 * QK_ROPE
    for h in range(N_HEADS):
        qh = q[:, h * per_head:(h + 1) * per_head]
        qlat = _dot(qh[:, :QK_NOPE].astype(BF16), wk_ref[h]) * (SM_SCALE * LOG2_E)
        qpe = (qh[:, QK_NOPE:QK_NOPE + QK_ROPE] * cos + qh[:, QK_NOPE + QK_ROPE:] * sin) * (SM_SCALE * LOG2_E)
        qlat_ref[h] = qlat.astype(BF16)
        qpe_ref[h] = qpe.astype(BF16)


def _latent_query(x, cos, sin, wkva, kvn, wqa, qn, wqb, wk):
    rows = x.shape[0]
    tile = min(ROW_TILE, rows)
    pos_tiles = cos.shape[0] // tile
    row = lambda width: pl.BlockSpec((tile, width), lambda i: (i, 0))
    pos = pl.BlockSpec((tile, QK_ROPE), lambda i: (i % pos_tiles, 0))
    head = lambda width: pl.BlockSpec((N_HEADS, tile, width), lambda i: (0, i, 0))
    return pl.pallas_call(
        _latent_query_kernel,
        grid=(rows // tile,),
        in_specs=[row(D_MODEL), pos, pos, _const_spec(wkva.shape), _const_spec(kvn.shape),
                  _const_spec(wqa.shape), _const_spec(qn.shape), _const_spec(wqb.shape),
                  _const_spec(wk.shape)],
        out_specs=[row(KV_LORA), row(QK_ROPE), row(KV_LORA), row(QK_ROPE), head(KV_LORA), head(QK_ROPE)],
        out_shape=[jax.ShapeDtypeStruct((rows, KV_LORA), F32),
                   jax.ShapeDtypeStruct((rows, QK_ROPE), F32),
                   jax.ShapeDtypeStruct((rows, KV_LORA), BF16),
                   jax.ShapeDtypeStruct((rows, QK_ROPE), BF16),
                   jax.ShapeDtypeStruct((N_HEADS, rows, KV_LORA), BF16),
                   jax.ShapeDtypeStruct((N_HEADS, rows, QK_ROPE), BF16)],
        compiler_params=_params(1),
        name="latent_query",
    )(x, cos, sin, wkva, kvn, wqa, qn, wqb, wk)


def _lane_tile(v, width):
    return jnp.concatenate([v] * (width // LANES), axis=1)


def _flash_kernel(ql_ref, qp_ref, kl_ref, kp_ref, ml_ref, mp_ref, wv_ref, o_ref, m_sc, l_sc, acc_sc):
    tq = ql_ref.shape[1]
    i = pl.program_id(1)
    units = [(h, pl.ds(r, FLASH_UNIT_ROWS)) for h in range(N_HEADS) for r in range(0, tq, FLASH_UNIT_ROWS)]

    def scores(u, keys):
        h, rows = units[u]
        return [_dot_nt(ql_ref[h, rows, :], kl) + _dot_nt(qp_ref[h, rows, :], kp) for kl, kp in keys]

    def softmax_part(u, s_list, first):
        h, rows = units[u]
        m_cur = jnp.max(s_list[0], axis=1, keepdims=True)
        for s in s_list[1:]:
            m_cur = jnp.maximum(m_cur, jnp.max(s, axis=1, keepdims=True))
        if first:
            m_new = jnp.broadcast_to(m_cur, (FLASH_UNIT_ROWS, LANES))
            alpha = None
        else:
            m_prev = m_sc[h, rows, :]
            m_new = jnp.maximum(m_prev, m_cur)
            alpha = jnp.exp2(m_prev - m_new)
        p_list = [jnp.exp2(s - _lane_tile(m_new, s.shape[1])) for s in s_list]
        p_sum = None
        for p in p_list:
            for c in range(p.shape[1] // LANES):
                blk = p[:, c * LANES:(c + 1) * LANES]
                p_sum = blk if p_sum is None else p_sum + blk
        l_sc[h, rows, :] = p_sum if first else alpha * l_sc[h, rows, :] + p_sum
        m_sc[h, rows, :] = m_new
        return [p.astype(BF16) for p in p_list], alpha

    def value_part(u, p_list, alpha, keys, first):
        h, rows = units[u]
        pv = None
        for p, (kl, _) in zip(p_list, keys):
            d = _dot(p, kl)
            pv = d if pv is None else pv + d
        acc_sc[h, rows, :] = pv if first else _lane_tile(alpha, KV_LORA) * acc_sc[h, rows, :] + pv

    def step(keys, masks, first):
        s_q, p_q = {}, {}
        for t in range(len(units) + FLASH_VALUE_LAG):
            if t < len(units):
                s_q[t] = scores(t, keys)
            u = t - FLASH_SOFTMAX_LAG
            if 0 <= u < len(units):
                s_list = [s if mk is None else jnp.where(mk(units[u][1].start, s.shape), s, MASK_VALUE)
                          for s, mk in zip(s_q.pop(u), masks)]
                p_q[u] = softmax_part(u, s_list, first)
            u = t - FLASH_VALUE_LAG
            if 0 <= u < len(units):
                value_part(u, *p_q.pop(u), keys, first)

    def key_tile(start, size):
        off = pl.multiple_of(start, tq)
        return kl_ref[pl.ds(off, size), :], kp_ref[pl.ds(off, size), :]

    meta_mask = lambda row0, shape: lax.broadcasted_iota(jnp.int32, shape, 1) < N_META
    causal_mask = lambda row0, shape: (lax.broadcasted_iota(jnp.int32, shape, 1)
                                       <= lax.broadcasted_iota(jnp.int32, shape, 0) + row0)
    step([(ml_ref[...], mp_ref[...]), key_tile(i * tq, tq)], [meta_mask, causal_mask], True)

    wide = FLASH_KEY_TILE // tq

    def body(j, carry):
        step([key_tile(j * FLASH_KEY_TILE, FLASH_KEY_TILE)], [None], False)
        return carry

    lax.fori_loop(0, i // wide, body, 0)
    for r in range(wide - 1):
        @pl.when(i % wide > r)
        def _():
            step([key_tile((i // wide * wide + r) * tq, tq)], [None], False)

    for h in range(N_HEADS):
        l_row = jnp.sum(l_sc[h], axis=1, keepdims=True)
        o = (acc_sc[h] / l_row).astype(BF16)
        o_ref[:, h * V_HEAD:(h + 1) * V_HEAD] = _dot(o, wv_ref[h]).astype(BF16)


def _flash(qlat, qpe, klat, kpe, mlat, mpe, wv, bsz):
    rows = klat.shape[0]
    seq = rows // bsz
    tq = ATTN_TILE
    assert tq & (tq - 1) == 0 and seq % tq == 0
    nq = seq // tq
    return pl.pallas_call(
        _flash_kernel,
        grid=(bsz, nq),
        in_specs=[pl.BlockSpec((N_HEADS, tq, KV_LORA), lambda b, i: (0, b * nq + i, 0)),
                  pl.BlockSpec((N_HEADS, tq, QK_ROPE), lambda b, i: (0, b * nq + i, 0)),
                  pl.BlockSpec((seq, KV_LORA), lambda b, i: (b, 0)),
                  pl.BlockSpec((seq, QK_ROPE), lambda b, i: (b, 0)),
                  _const_spec(mlat.shape), _const_spec(mpe.shape), _const_spec(wv.shape)],
        out_specs=pl.BlockSpec((tq, N_HEADS * V_HEAD), lambda b, i: (b * nq + i, 0)),
        out_shape=jax.ShapeDtypeStruct((rows, N_HEADS * V_HEAD), BF16),
        scratch_shapes=[pltpu.VMEM((N_HEADS, tq, LANES), F32), pltpu.VMEM((N_HEADS, tq, LANES), F32),
                        pltpu.VMEM((N_HEADS, tq, KV_LORA), F32)],
        compiler_params=_params(2),
        name="flash_prompt",
    )(qlat, qpe, klat, kpe, mlat, mpe, wv)


def _decode_kernel(pt_ref, ql_ref, qp_ref, kn_ref, pn_ref, ckv_hbm, kpe_hbm, o_ref, kbuf, pbuf, sem,
                   *, n_chunks):
    b = pl.program_id(0)
    n_samples = pl.num_programs(0)

    def chunk_copies(sample, c, slot):
        copies = []
        for p in range(PAGES_PER_CHUNK):
            page = pt_ref[sample, c * PAGES_PER_CHUNK + p]
            dst = pl.ds(p * PAGE_SIZE, PAGE_SIZE)
            copies.append(pltpu.make_async_copy(ckv_hbm.at[page], kbuf.at[slot, dst, :], sem.at[0, slot]))
            copies.append(pltpu.make_async_copy(kpe_hbm.at[page], pbuf.at[slot, :, dst], sem.at[1, slot]))
        return copies

    @pl.when(b == 0)
    def _():
        for cp in chunk_copies(0, 0, 0):
            cp.start()

    ql = ql_ref[0]
    qp = qp_ref[0]
    kn = kn_ref[0].astype(F32)
    pn = pn_ref[0].astype(F32)
    m_run = (jnp.sum(ql.astype(F32) * kn, axis=1, keepdims=True)
             + jnp.sum(qp.astype(F32) * pn, axis=1, keepdims=True))
    l_run = jnp.ones_like(m_run)
    acc = jnp.broadcast_to(kn, (N_HEADS, KV_LORA))

    for c in range(n_chunks):
        slot = c % 2
        if c + 1 < n_chunks:
            for cp in chunk_copies(b, c + 1, 1 - slot):
                cp.start()
        else:
            @pl.when(b + 1 < n_samples)
            def _():
                for cp in chunk_copies(b + 1, 0, 1 - slot):
                    cp.start()
        for cp in chunk_copies(b, c, slot):
            cp.wait()
        k = kbuf[slot].astype(BF16)
        pe = pbuf[slot].astype(BF16)
        s = _dot_nt(ql, k) + _dot(qp, pe)
        m_new = jnp.maximum(m_run, jnp.max(s, axis=1, keepdims=True))
        alpha = jnp.exp2(m_run - m_new)
        p = jnp.exp2(s - m_new)
        l_run = alpha * l_run + jnp.sum(p, axis=1, keepdims=True)
        acc = alpha * acc + _dot(p.astype(BF16), k)
        m_run = m_new
    o_ref[0] = acc / l_run


def _decode_attention(page_table, qlat, qpe, knew, pnew, cache_ckv, cache_kpe):
    n_samples, n_pages = page_table.shape
    chunk_rows = PAGES_PER_CHUNK * PAGE_SIZE
    assert n_pages % (2 * PAGES_PER_CHUNK) == 0
    per_sample = lambda width, lead: pl.BlockSpec((1, lead, width), lambda b, pt: (b, 0, 0))
    grid_spec = pltpu.PrefetchScalarGridSpec(
        num_scalar_prefetch=1,
        grid=(n_samples,),
        in_specs=[per_sample(KV_LORA, N_HEADS), per_sample(QK_ROPE, N_HEADS),
                  per_sample(KV_LORA, 1), per_sample(QK_ROPE, 1),
                  pl.BlockSpec(memory_space=pl.ANY), pl.BlockSpec(memory_space=pl.ANY)],
        out_specs=per_sample(KV_LORA, N_HEADS),
        scratch_shapes=[pltpu.VMEM((2, chunk_rows, KV_LORA), F32),
                        pltpu.VMEM((2, QK_ROPE, chunk_rows), F32),
                        pltpu.SemaphoreType.DMA((2, 2))],
    )
    return pl.pallas_call(
        functools.partial(_decode_kernel, n_chunks=n_pages // PAGES_PER_CHUNK),
        grid_spec=grid_spec,
        out_shape=jax.ShapeDtypeStruct((n_samples, N_HEADS, KV_LORA), F32),
        compiler_params=_params(1),
        name="decode_attention",
    )(page_table, qlat, qpe, knew, pnew, cache_ckv, cache_kpe)


def _value_proj_kernel(o_ref, wv_ref, a_ref):
    for h in range(N_HEADS):
        a_ref[:, h * V_HEAD:(h + 1) * V_HEAD] = _dot(o_ref[h].astype(BF16), wv_ref[h]).astype(BF16)


def _value_proj(o, wv):
    rows = o.shape[1]
    return pl.pallas_call(
        _value_proj_kernel,
        grid=(1,),
        in_specs=[_const_spec(o.shape), _const_spec(wv.shape)],
        out_specs=_const_spec((rows, N_HEADS * V_HEAD)),
        out_shape=jax.ShapeDtypeStruct((rows, N_HEADS * V_HEAD), BF16),
        compiler_params=_params(1),
        name="value_proj",
    )(o, wv)


def _rope_tables(pos):
    half = QK_ROPE // 2
    inv = ROPE_THETA ** (-jnp.arange(half, dtype=F32) / half)
    ang = pos.astype(F32)[:, None] * inv[None, :]
    cos, sin = jnp.cos(ang), jnp.sin(ang)
    return jnp.concatenate([cos, cos], axis=1), jnp.concatenate([-sin, sin], axis=1)


def _rotate_half_cols(w):
    half = QK_ROPE // 2
    return jnp.concatenate([w[..., half:], w[..., :half]], axis=-1)


def kernel(x_prompt, x_sample, state_pool, cache_ckv, cache_kpe, page_table, meta_tokens,
           ln_g, ln_b, pool_w, pool_scale, w_gate, w_up, w_down,
           wq_a, q_norm, wq_b, wo, wkv_a, kv_norm, wkv_b):
    bsz, seq, _ = x_prompt.shape
    n_samples = x_sample.shape[0]
    past_len = page_table.shape[1] * PAGE_SIZE

    vec = lambda v: v.reshape(1, -1).astype(F32)
    pool_wb = pool_w[0].astype(BF16)
    wg, wu, wd = w_gate.astype(BF16), w_up.astype(BF16), w_down.astype(BF16)
    wkva = jnp.concatenate([wkv_a, _rotate_half_cols(wkv_a[:, KV_LORA:])], axis=1).astype(BF16)
    wqb3 = wq_b[0].reshape(wq_b.shape[1], N_HEADS, QK_NOPE + QK_ROPE)
    wqb = jnp.concatenate([wqb3, _rotate_half_cols(wqb3[..., QK_NOPE:])], axis=-1)
    wqb = wqb.reshape(wq_b.shape[1], -1).astype(BF16)
    wk = jnp.transpose(wkv_b[:, :, :QK_NOPE], (1, 2, 0)).astype(BF16)
    wv = jnp.transpose(wkv_b[:, :, QK_NOPE:], (1, 0, 2)).astype(BF16)
    wqa, wob = wq_a[0].astype(BF16), wo[0].astype(BF16)
    ln = lambda l, k: (vec(ln_g[l, k]), vec(ln_b[l, k]))

    meta = meta_tokens.astype(F32)
    pool_args = (pool_wb, vec(pool_scale[0])) + ln(0, 0)
    x1_main = _pool_main(x_prompt, meta, *pool_args).reshape(bsz * seq, D_MODEL)
    x1_meta = _pool_meta(meta, *pool_args)
    xs_sample = jnp.concatenate([jnp.swapaxes(state_pool[0], 0, 1), jnp.swapaxes(x_sample, 0, 1)], axis=0)
    x1_sample = _pool_sample(xs_sample, *pool_args)
    x1_small = jnp.concatenate([x1_sample, x1_meta], axis=0)

    ffn0 = (wg[0], wu[0], wd[0]) + ln(0, 1)
    x2_main = _ffn(x1_main, *ffn0)
    x2_small = _ffn(x1_small, *ffn0)

    cos_main, sin_main = _rope_tables(N_META + jnp.arange(seq))
    pos_small = jnp.concatenate([jnp.full((n_samples,), past_len), jnp.arange(N_META)])
    cos_small, sin_small = _rope_tables(pos_small)
    lq_w = (wkva, vec(kv_norm), wqa, vec(q_norm[0]), wqb, wk)
    ckv_main, kpe_main, klat_main, kpeb_main, qlat_main, qpe_main = _latent_query(
        x2_main, cos_main, sin_main, *lq_w)
    ckv_small, kpe_small, klat_small, kpeb_small, qlat_small, qpe_small = _latent_query(
        x2_small, cos_small, sin_small, *lq_w)

    pad_keys = lambda k: jnp.pad(k[n_samples:], ((0, LANES - N_META), (0, 0)))
    a_main = _flash(qlat_main, qpe_main, klat_main, kpeb_main,
                    pad_keys(klat_small), pad_keys(kpeb_small), wv, bsz)
    o_sample = _decode_attention(
        page_table,
        jnp.swapaxes(qlat_small[:, :n_samples], 0, 1), jnp.swapaxes(qpe_small[:, :n_samples], 0, 1),
        klat_small[:n_samples, None, :], kpeb_small[:n_samples, None, :],
        cache_ckv, jnp.swapaxes(cache_kpe, 1, 2))
    a_sample = _value_proj(jnp.swapaxes(o_sample, 0, 1), wv)

    ffn1 = (wg[1], wu[1], wd[1]) + ln(1, 1)
    y_main = _ffn(x2_main, *ffn1, proj=(a_main, wob) + ln(1, 0))
    y_sample = _ffn(x2_small[:n_samples], *ffn1, proj=(a_sample, wob) + ln(1, 0))

    pool_prompt = x_prompt[None, :, seq - POOL_HIST:, :]
    pool_sample = jnp.concatenate([state_pool[0][:, 1:], x_sample], axis=1)[None]
    bcast = lambda t: jnp.broadcast_to(t[n_samples:][None], (bsz, N_META, t.shape[1]))
    ckv_prompt = jnp.concatenate([bcast(ckv_small), ckv_main.reshape(bsz, seq, KV_LORA)], axis=1)
    kpe_prompt = jnp.concatenate([bcast(kpe_small), kpe_main.reshape(bsz, seq, QK_ROPE)], axis=1)
    return (y_main.reshape(bsz, seq, D_MODEL), y_sample.reshape(n_samples, 1, D_MODEL),
            pool_prompt, pool_sample, ckv_prompt, kpe_prompt,
            ckv_small[:n_samples, None, :], kpe_small[:n_samples, None, :])
```

```python
import functools

import jax
import jax.numpy as jnp
from jax import lax
from jax.experimental import pallas as pl
from jax.experimental.pallas import tpu as pltpu

F32 = jnp.float32
BF16 = jnp.bfloat16

D_MODEL = 1024
DEPTH = 2
N_META = 16
POOL_WINDOWS = (2, 4, 8, 16)
POOL_GROUP = D_MODEL // len(POOL_WINDOWS)
POOL_HIST = max(POOL_WINDOWS) - 1
N_HEADS = 8
QK_NOPE = 128
QK_ROPE = 64
V_HEAD = 128
KV_LORA = 256
ROPE_THETA = 10000.0
ALPHA = (2 * DEPTH) ** 0.25
LN_EPS = 1e-5
RMS_EPS = 1e-6
SM_SCALE = (QK_NOPE + QK_ROPE) ** -0.5
PAGE_SIZE = 128

LANES = 128
MASK_VALUE = -1e30
VMEM_LIMIT = 48 * 1024 * 1024

ROW_TILE = 512
ATTN_TILE = 512
FLASH_UNIT_ROWS = 256
FLASH_SOFTMAX_LAG = 1
FLASH_VALUE_LAG = 2
LOG2_E = 1.4426950408889634
PAGES_PER_CHUNK = 32
DECODE_SLOTS = 3


def _dot(a, b):
    return jnp.dot(a, b, preferred_element_type=F32)


def _dot_nt(a, b):
    return lax.dot_general(a, b, (((1,), (1,)), ((), ())), preferred_element_type=F32)


def _layer_norm(x, g, b):
    mu = jnp.mean(x, axis=-1, keepdims=True)
    xc = x - mu
    var = jnp.mean(xc * xc, axis=-1, keepdims=True)
    return xc * lax.rsqrt(var + LN_EPS) * g + b


def _rms_norm(x, g):
    return x * lax.rsqrt(jnp.mean(x * x, axis=-1, keepdims=True) + RMS_EPS) * g


def _const_spec(shape):
    zeros = (0,) * len(shape)
    return pl.BlockSpec(shape, lambda *_: zeros, pipeline_mode=pl.Buffered(1))


def _params(n_axes):
    return pltpu.CompilerParams(dimension_semantics=("arbitrary",) * n_axes,
                                vmem_limit_bytes=VMEM_LIMIT)


def _window_sums(xs, n_rows, pad):
    hi = xs.astype(BF16)
    lo = (xs - hi.astype(F32)).astype(BF16)
    k = xs.shape[0]
    diff = (lax.broadcasted_iota(jnp.int32, (n_rows, k), 0) + pad
            - lax.broadcasted_iota(jnp.int32, (n_rows, k), 1))
    sums = []
    for g, win in enumerate(POOL_WINDOWS):
        band = jnp.where(diff >= 0, jnp.where(diff < win, 1.0, 0.0), 0.0).astype(BF16)
        cols = slice(g * POOL_GROUP, (g + 1) * POOL_GROUP)
        sums.append(_dot(band, hi[:, cols]) + _dot(band, lo[:, cols]))
    return sums


def _pool_finish(x, means, w_ref, scale, g, b):
    outs = []
    for gi in range(len(POOL_WINDOWS)):
        cols = slice(gi * POOL_GROUP, (gi + 1) * POOL_GROUP)
        d = means[gi] - x[:, cols]
        outs.append(_dot(d.astype(BF16), w_ref[gi]))
    h = jnp.concatenate(outs, axis=1) * scale
    return _layer_norm(ALPHA * x + h, g, b)


def _blocked_window_means(xs, n_rows):
    hi = xs.astype(BF16)
    lo = (xs - hi.astype(F32)).astype(BF16)
    diff = (lax.broadcasted_iota(jnp.int32, (LANES, 2 * LANES), 0) + LANES
            - lax.broadcasted_iota(jnp.int32, (LANES, 2 * LANES), 1))
    means = []
    for g, win in enumerate(POOL_WINDOWS):
        band = jnp.where(diff >= 0, jnp.where(diff < win, 1.0, 0.0), 0.0).astype(BF16)
        cols = slice(g * POOL_GROUP, (g + 1) * POOL_GROUP)
        blocks = []
        for r in range(0, n_rows, LANES):
            rows = slice(r, r + 2 * LANES)
            blocks.append(_dot(band, hi[rows, cols]) + _dot(band, lo[rows, cols]))
        means.append(jnp.concatenate(blocks, axis=0) * (1.0 / win))
    return means


def _pool_main_rows(x, prev, meta, is_first, w_ref, scale, g, b):
    halo = jnp.where(is_first, meta, prev)
    xs = jnp.concatenate([jnp.zeros((LANES - N_META, D_MODEL), F32), halo, x], axis=0)
    return _pool_finish(x, _blocked_window_means(xs, x.shape[0]), w_ref, scale, g, b)


def _pool_meta_kernel(x_ref, w_ref, scale_ref, g_ref, b_ref, o_ref):
    x = x_ref[...]
    n = x.shape[0]
    xs = jnp.concatenate([jnp.zeros((LANES - n, D_MODEL), F32), x], axis=0)
    sums = _window_sums(xs, n, LANES - n)
    row = lax.broadcasted_iota(jnp.int32, (n, 1), 0)
    means = [s / jnp.minimum(row + 1, w).astype(F32) for s, w in zip(sums, POOL_WINDOWS)]
    o_ref[...] = _pool_finish(x, means, w_ref, scale_ref[...], g_ref[...], b_ref[...])


def _pool_sample_kernel(xs_ref, w_ref, scale_ref, g_ref, b_ref, o_ref):
    n_ctx = xs_ref.shape[0]
    x = xs_ref[n_ctx - 1]
    means = []
    for gi, win in enumerate(POOL_WINDOWS):
        cols = slice(gi * POOL_GROUP, (gi + 1) * POOL_GROUP)
        s = xs_ref[n_ctx - win, :, cols]
        for k in range(n_ctx - win + 1, n_ctx):
            s = s + xs_ref[k, :, cols]
        means.append(s * (1.0 / win))
    o_ref[...] = _pool_finish(x, means, w_ref, scale_ref[...], g_ref[...], b_ref[...])


def _pool_weight_specs():
    return [_const_spec((len(POOL_WINDOWS), POOL_GROUP, POOL_GROUP)),
            _const_spec((1, D_MODEL)), _const_spec((1, D_MODEL)), _const_spec((1, D_MODEL))]


def _pool_meta(meta, pool_w, scale, g, b):
    return pl.pallas_call(
        _pool_meta_kernel,
        grid=(1,),
        in_specs=[_const_spec(meta.shape)] + _pool_weight_specs(),
        out_specs=_const_spec(meta.shape),
        out_shape=jax.ShapeDtypeStruct(meta.shape, F32),
        compiler_params=_params(1),
        name="pool_meta",
    )(meta, pool_w, scale, g, b)


def _pool_sample(xs, pool_w, scale, g, b):
    rows = xs.shape[1]
    return pl.pallas_call(
        _pool_sample_kernel,
        grid=(1,),
        in_specs=[_const_spec(xs.shape)] + _pool_weight_specs(),
        out_specs=_const_spec((rows, D_MODEL)),
        out_shape=jax.ShapeDtypeStruct((rows, D_MODEL), F32),
        compiler_params=_params(1),
        name="pool_sample",
    )(xs, pool_w, scale, g, b)


def _ffn_kernel(*refs, mixer, n_chunks, tiles_per_seq):
    if mixer == "proj":
        a_ref, x_ref, wo_ref, g1_ref, b1_ref = refs[:5]
        refs = refs[5:]
        x = _layer_norm(ALPHA * x_ref[...] + _dot(a_ref[...], wo_ref[...]), g1_ref[...], b1_ref[...])
    elif mixer == "pool":
        x_ref, prev_ref, meta_ref, pw_ref, scale_ref, g1_ref, b1_ref = refs[:7]
        refs = refs[7:]
        is_first = pl.program_id(0) % tiles_per_seq == 0
        x = _pool_main_rows(x_ref[...], prev_ref[...], meta_ref[...], is_first, pw_ref,
                            scale_ref[...], g1_ref[...], b1_ref[...])
    else:
        x_ref = refs[0]
        refs = refs[1:]
        x = x_ref[...]
    wg_ref, wu_ref, wd_ref, g2_ref, b2_ref, o_ref = refs
    xb = x.astype(BF16)
    chunk = wg_ref.shape[1] // n_chunks
    acc = None
    for c in range(n_chunks):
        cols = slice(c * chunk, (c + 1) * chunk)
        gate = _dot(xb, wg_ref[:, cols])
        up = _dot(xb, wu_ref[:, cols])
        mid = (gate * (1.0 / (1.0 + jnp.exp(-gate))) * up).astype(BF16)
        part = _dot(mid, wd_ref[cols, :])
        acc = part if acc is None else acc + part
    o_ref[...] = _layer_norm(ALPHA * x + acc, g2_ref[...], b2_ref[...])


def _ffn(x, wg, wu, wd, g2, b2, proj=None, pool=None):
    rows = x.shape[0]
    tile = min(ROW_TILE, rows)
    row_spec = pl.BlockSpec((tile, D_MODEL), lambda i: (i, 0))
    vec_spec = _const_spec((1, D_MODEL))
    args, specs = [], []
    mixer, tiles_per_seq = None, None
    if proj is not None:
        mixer = "proj"
        a, wo, g1, b1 = proj
        args += [a, x, wo, g1, b1]
        specs += [row_spec, row_spec, _const_spec(wo.shape), vec_spec, vec_spec]
    elif pool is not None:
        mixer = "pool"
        meta, pool_w, scale, g1, b1, seq = pool
        assert seq % tile == 0 and tile % LANES == 0
        tiles_per_seq = seq // tile
        halo_blocks = tile // N_META
        args += [x, x, meta, pool_w, scale, g1, b1]
        specs += [row_spec,
                  pl.BlockSpec((N_META, D_MODEL), lambda i: (jnp.maximum(i * halo_blocks - 1, 0), 0)),
                  _const_spec(meta.shape)] + _pool_weight_specs()
    else:
        args += [x]
        specs += [row_spec]
    args += [wg, wu, wd, g2, b2]
    specs += [_const_spec(wg.shape), _const_spec(wu.shape), _const_spec(wd.shape), vec_spec, vec_spec]
    return pl.pallas_call(
        functools.partial(_ffn_kernel, mixer=mixer, n_chunks=2, tiles_per_seq=tiles_per_seq),
        grid=(rows // tile,),
        in_specs=specs,
        out_specs=row_spec,
        out_shape=jax.ShapeDtypeStruct((rows, D_MODEL), F32),
        compiler_params=_params(1),
        name="ffn" if mixer is None else "ffn_" + mixer,
    )(*args)


def _latent_query_kernel(x_ref, cos_ref, sin_ref, wkva_ref, kvn_ref, wqa_ref, qn_ref, wqb_ref, wk_ref,
                         ckv_ref, kpe_ref, klat_ref, kpeb_ref, qlat_ref, qpe_ref):
    xb = x_ref[...].astype(BF16)
    cos = cos_ref[...]
    sin = sin_ref[...]
    kv = _dot(xb, wkva_ref[...])
    ckv = _rms_norm(kv[:, :KV_LORA], kvn_ref[...])
    kpe = kv[:, KV_LORA:KV_LORA + QK_ROPE] * cos + kv[:, KV_LORA + QK_ROPE:] * sin
    ckv_ref[...] = ckv
    kpe_ref[...] = kpe
    klat_ref[...] = ckv.astype(BF16)
    kpeb_ref[...] = kpe.astype(BF16)
    qa = _rms_norm(_dot(xb, wqa_ref[...]), qn_ref[...]).astype(BF16)
    q = _dot(qa, wqb_ref[...])
    per_head = QK_NOPE + 2 * QK_ROPE
    for h in range(N_HEADS):
        qh = q[:, h * per_head:(h + 1) * per_head]
        qlat = _dot(qh[:, :QK_NOPE].astype(BF16), wk_ref[h]) * (SM_SCALE * LOG2_E)
        qpe = (qh[:, QK_NOPE:QK_NOPE + QK_ROPE] * cos + qh[:, QK_NOPE + QK_ROPE:] * sin) * (SM_SCALE * LOG2_E)
        qlat_ref[h] = qlat.astype(BF16)
        qpe_ref[h] = qpe.astype(BF16)


def _latent_query(x, cos, sin, wkva, kvn, wqa, qn, wqb, wk):
    rows = x.shape[0]
    tile = min(ROW_TILE, rows)
    pos_tiles = cos.shape[0] // tile
    row = lambda width: pl.BlockSpec((tile, width), lambda i: (i, 0))
    pos = pl.BlockSpec((tile, QK_ROPE), lambda i: (i % pos_tiles, 0))
    head = lambda width: pl.BlockSpec((N_HEADS, tile, width), lambda i: (0, i, 0))
    return pl.pallas_call(
        _latent_query_kernel,
        grid=(rows // tile,),
        in_specs=[row(D_MODEL), pos, pos, _const_spec(wkva.shape), _const_spec(kvn.shape),
                  _const_spec(wqa.shape), _const_spec(qn.shape), _const_spec(wqb.shape),
                  _const_spec(wk.shape)],
        out_specs=[row(KV_LORA), row(QK_ROPE), row(KV_LORA), row(QK_ROPE), head(KV_LORA), head(QK_ROPE)],
        out_shape=[jax.ShapeDtypeStruct((rows, KV_LORA), F32),
                   jax.ShapeDtypeStruct((rows, QK_ROPE), F32),
                   jax.ShapeDtypeStruct((rows, KV_LORA), BF16),
                   jax.ShapeDtypeStruct((rows, QK_ROPE), BF16),
                   jax.ShapeDtypeStruct((N_HEADS, rows, KV_LORA), BF16),
                   jax.ShapeDtypeStruct((N_HEADS, rows, QK_ROPE), BF16)],
        compiler_params=_params(1),
        name="latent_query",
    )(x, cos, sin, wkva, kvn, wqa, qn, wqb, wk)


def _lane_tile(v, width):
    return jnp.concatenate([v] * (width // LANES), axis=1)


def _flash_kernel(ql_ref, qp_ref, kl_ref, kp_ref, ml_ref, mp_ref, wv_ref, o_ref, m_sc, l_sc, acc_sc):
    tq = ql_ref.shape[1]
    i = pl.program_id(1)
    units = [(h, pl.ds(r, FLASH_UNIT_ROWS)) for h in range(N_HEADS) for r in range(0, tq, FLASH_UNIT_ROWS)]

    def scores(u, keys):
        h, rows = units[u]
        return [_dot_nt(ql_ref[h, rows, :], kl) + _dot_nt(qp_ref[h, rows, :], kp) for kl, kp in keys]

    def softmax_part(u, s_list, first):
        h, rows = units[u]
        m_cur = jnp.max(s_list[0], axis=1, keepdims=True)
        for s in s_list[1:]:
            m_cur = jnp.maximum(m_cur, jnp.max(s, axis=1, keepdims=True))
        if first:
            m_new = jnp.broadcast_to(m_cur, (FLASH_UNIT_ROWS, LANES))
            alpha = None
        else:
            m_prev = m_sc[h, rows, :]
            m_new = jnp.maximum(m_prev, m_cur)
            alpha = jnp.exp2(m_prev - m_new)
        p_list = [jnp.exp2(s - _lane_tile(m_new, s.shape[1])) for s in s_list]
        p_sum = None
        for p in p_list:
            for c in range(p.shape[1] // LANES):
                blk = p[:, c * LANES:(c + 1) * LANES]
                p_sum = blk if p_sum is None else p_sum + blk
        l_sc[h, rows, :] = p_sum if first else alpha * l_sc[h, rows, :] + p_sum
        m_sc[h, rows, :] = m_new
        return [p.astype(BF16) for p in p_list], alpha

    def value_part(u, p_list, alpha, keys, first):
        h, rows = units[u]
        pv = None
        for p, (kl, _) in zip(p_list, keys):
            d = _dot(p, kl)
            pv = d if pv is None else pv + d
        acc_sc[h, rows, :] = pv if first else _lane_tile(alpha, KV_LORA) * acc_sc[h, rows, :] + pv

    def step(unit_keys, first):
        s_q, p_q = {}, {}
        for t in range(len(units) + FLASH_VALUE_LAG):
            if t < len(units):
                s_q[t] = scores(t, unit_keys(t)[0])
            u = t - FLASH_SOFTMAX_LAG
            if 0 <= u < len(units):
                s_list = [s if mk is None else jnp.where(mk(s.shape), s, MASK_VALUE)
                          for s, mk in zip(s_q.pop(u), unit_keys(u)[1])]
                p_q[u] = softmax_part(u, s_list, first)
            u = t - FLASH_VALUE_LAG
            if 0 <= u < len(units):
                value_part(u, *p_q.pop(u), unit_keys(u)[0], first)

    def key_tile(start, size):
        off = pl.multiple_of(start, FLASH_UNIT_ROWS)
        return kl_ref[pl.ds(off, size), :], kp_ref[pl.ds(off, size), :]

    meta_mask = lambda shape: lax.broadcasted_iota(jnp.int32, shape, 1) < N_META
    causal_mask = lambda shape: (lax.broadcasted_iota(jnp.int32, shape, 1)
                                 <= lax.broadcasted_iota(jnp.int32, shape, 0))
    meta_keys = (ml_ref[...], mp_ref[...])
    diag_tiles = {r: key_tile(i * tq + r, FLASH_UNIT_ROWS) for r in range(0, tq, FLASH_UNIT_ROWS)}

    def first_keys(u):
        row0 = units[u][1].start
        below = list(range(0, row0, FLASH_UNIT_ROWS))
        return ([meta_keys] + [diag_tiles[r] for r in below] + [diag_tiles[row0]],
                [meta_mask] + [None] * len(below) + [causal_mask])

    step(first_keys, True)

    def body(j, carry):
        full_tile = key_tile(j * tq, tq)
        step(lambda u: ([full_tile], [None]), False)
        return carry

    lax.fori_loop(0, i, body, 0)

    for h in range(N_HEADS):
        l_row = jnp.sum(l_sc[h], axis=1, keepdims=True)
        o = (acc_sc[h] / l_row).astype(BF16)
        o_ref[:, h * V_HEAD:(h + 1) * V_HEAD] = _dot(o, wv_ref[h]).astype(BF16)


def _flash(qlat, qpe, klat, kpe, mlat, mpe, wv, bsz):
    rows = klat.shape[0]
    seq = rows // bsz
    tq = ATTN_TILE
    assert seq % tq == 0 and tq % FLASH_UNIT_ROWS == 0
    nq = seq // tq
    return pl.pallas_call(
        _flash_kernel,
        grid=(bsz, nq),
        in_specs=[pl.BlockSpec((N_HEADS, tq, KV_LORA), lambda b, i: (0, b * nq + i, 0)),
                  pl.BlockSpec((N_HEADS, tq, QK_ROPE), lambda b, i: (0, b * nq + i, 0)),
                  pl.BlockSpec((seq, KV_LORA), lambda b, i: (b, 0)),
                  pl.BlockSpec((seq, QK_ROPE), lambda b, i: (b, 0)),
                  _const_spec(mlat.shape), _const_spec(mpe.shape), _const_spec(wv.shape)],
        out_specs=pl.BlockSpec((tq, N_HEADS * V_HEAD), lambda b, i: (b * nq + i, 0)),
        out_shape=jax.ShapeDtypeStruct((rows, N_HEADS * V_HEAD), BF16),
        scratch_shapes=[pltpu.VMEM((N_HEADS, tq, LANES), F32), pltpu.VMEM((N_HEADS, tq, LANES), F32),
                        pltpu.VMEM((N_HEADS, tq, KV_LORA), F32)],
        compiler_params=_params(2),
        name="flash_prompt",
    )(qlat, qpe, klat, kpe, mlat, mpe, wv)


def _decode_kernel(pt_ref, ql_ref, qp_ref, kn_ref, pn_ref, ckv_hbm, kpe_hbm, o_ref, kbuf, pbuf, sem,
                   *, n_chunks):
    b = pl.program_id(0)
    n_samples = pl.num_programs(0)
    ahead = DECODE_SLOTS - 1

    def chunk_copies(sample, c):
        slot = (sample * n_chunks + c) % DECODE_SLOTS
        copies = []
        for p in range(PAGES_PER_CHUNK):
            page = pt_ref[sample, c * PAGES_PER_CHUNK + p]
            dst = pl.ds(p * PAGE_SIZE, PAGE_SIZE)
            copies.append(pltpu.make_async_copy(ckv_hbm.at[page], kbuf.at[slot, dst, :], sem.at[0, slot]))
            copies.append(pltpu.make_async_copy(kpe_hbm.at[page], pbuf.at[slot, :, dst], sem.at[1, slot]))
        return slot, copies

    @pl.when(b == 0)
    def _():
        for c in range(ahead):
            for cp in chunk_copies(0, c)[1]:
                cp.start()

    ql = ql_ref[0]
    qp = qp_ref[0]
    kn = kn_ref[0].astype(F32)
    pn = pn_ref[0].astype(F32)
    m_run = (jnp.sum(ql.astype(F32) * kn, axis=1, keepdims=True)
             + jnp.sum(qp.astype(F32) * pn, axis=1, keepdims=True))
    l_run = jnp.ones_like(m_run)
    acc = jnp.broadcast_to(kn, (N_HEADS, KV_LORA))

    for c in range(n_chunks):
        nxt_sample, nxt_c = b + (c + ahead) // n_chunks, (c + ahead) % n_chunks
        if c + ahead < n_chunks:
            for cp in chunk_copies(nxt_sample, nxt_c)[1]:
                cp.start()
        else:
            @pl.when(nxt_sample < n_samples)
            def _():
                for cp in chunk_copies(nxt_sample, nxt_c)[1]:
                    cp.start()
        slot, copies = chunk_copies(b, c)
        for cp in copies:
            cp.wait()
        k = kbuf[slot].astype(BF16)
        pe = pbuf[slot].astype(BF16)
        s = _dot_nt(ql, k) + _dot(qp, pe)
        m_new = jnp.maximum(m_run, jnp.max(s, axis=1, keepdims=True))
        alpha = jnp.exp2(m_run - m_new)
        p = jnp.exp2(s - m_new)
        l_run = alpha * l_run + jnp.sum(p, axis=1, keepdims=True)
        acc = alpha * acc + _dot(p.astype(BF16), k)
        m_run = m_new
    o_ref[0] = acc / l_run


def _decode_attention(page_table, qlat, qpe, knew, pnew, cache_ckv, cache_kpe):
    n_samples, n_pages = page_table.shape
    chunk_rows = PAGES_PER_CHUNK * PAGE_SIZE
    n_chunks = n_pages // PAGES_PER_CHUNK
    assert n_pages % PAGES_PER_CHUNK == 0 and n_chunks >= DECODE_SLOTS - 1
    per_sample = lambda width, lead: pl.BlockSpec((1, lead, width), lambda b, pt: (b, 0, 0))
    grid_spec = pltpu.PrefetchScalarGridSpec(
        num_scalar_prefetch=1,
        grid=(n_samples,),
        in_specs=[per_sample(KV_LORA, N_HEADS), per_sample(QK_ROPE, N_HEADS),
                  per_sample(KV_LORA, 1), per_sample(QK_ROPE, 1),
                  pl.BlockSpec(memory_space=pl.ANY), pl.BlockSpec(memory_space=pl.ANY)],
        out_specs=per_sample(KV_LORA, N_HEADS),
        scratch_shapes=[pltpu.VMEM((DECODE_SLOTS, chunk_rows, KV_LORA), F32),
                        pltpu.VMEM((DECODE_SLOTS, QK_ROPE, chunk_rows), F32),
                        pltpu.SemaphoreType.DMA((2, DECODE_SLOTS))],
    )
    return pl.pallas_call(
        functools.partial(_decode_kernel, n_chunks=n_chunks),
        grid_spec=grid_spec,
        out_shape=jax.ShapeDtypeStruct((n_samples, N_HEADS, KV_LORA), F32),
        compiler_params=_params(1),
        name="decode_attention",
    )(page_table, qlat, qpe, knew, pnew, cache_ckv, cache_kpe)


def _value_proj_kernel(o_ref, wv_ref, a_ref):
    for h in range(N_HEADS):
        a_ref[:, h * V_HEAD:(h + 1) * V_HEAD] = _dot(o_ref[h].astype(BF16), wv_ref[h]).astype(BF16)


def _value_proj(o, wv):
    rows = o.shape[1]
    return pl.pallas_call(
        _value_proj_kernel,
        grid=(1,),
        in_specs=[_const_spec(o.shape), _const_spec(wv.shape)],
        out_specs=_const_spec((rows, N_HEADS * V_HEAD)),
        out_shape=jax.ShapeDtypeStruct((rows, N_HEADS * V_HEAD), BF16),
        compiler_params=_params(1),
        name="value_proj",
    )(o, wv)


def _rope_tables(pos):
    half = QK_ROPE // 2
    inv = ROPE_THETA ** (-jnp.arange(half, dtype=F32) / half)
    ang = pos.astype(F32)[:, None] * inv[None, :]
    cos, sin = jnp.cos(ang), jnp.sin(ang)
    return jnp.concatenate([cos, cos], axis=1), jnp.concatenate([-sin, sin], axis=1)


def _rotate_half_cols(w):
    half = QK_ROPE // 2
    return jnp.concatenate([w[..., half:], w[..., :half]], axis=-1)


def kernel(x_prompt, x_sample, state_pool, cache_ckv, cache_kpe, page_table, meta_tokens,
           ln_g, ln_b, pool_w, pool_scale, w_gate, w_up, w_down,
           wq_a, q_norm, wq_b, wo, wkv_a, kv_norm, wkv_b):
    bsz, seq, _ = x_prompt.shape
    n_samples = x_sample.shape[0]
    past_len = page_table.shape[1] * PAGE_SIZE

    vec = lambda v: v.reshape(1, -1).astype(F32)
    pool_wb = pool_w[0].astype(BF16)
    ffn_w = lambda l: (w_gate[l].astype(BF16), w_up[l].astype(BF16), w_down[l].astype(BF16))
    wkva = jnp.concatenate([wkv_a, _rotate_half_cols(wkv_a[:, KV_LORA:])], axis=1).astype(BF16)
    wqb3 = wq_b[0].reshape(wq_b.shape[1], N_HEADS, QK_NOPE + QK_ROPE)
    wqb = jnp.concatenate([wqb3, _rotate_half_cols(wqb3[..., QK_NOPE:])], axis=-1)
    wqb = wqb.reshape(wq_b.shape[1], -1).astype(BF16)
    wk = jnp.transpose(wkv_b[:, :, :QK_NOPE], (1, 2, 0)).astype(BF16)
    wv = jnp.transpose(wkv_b[:, :, QK_NOPE:], (1, 0, 2)).astype(BF16)
    wqa, wob = wq_a[0].astype(BF16), wo[0].astype(BF16)
    ln = lambda l, k: (vec(ln_g[l, k]), vec(ln_b[l, k]))

    meta = meta_tokens.astype(F32)
    pool_args = (pool_wb, vec(pool_scale[0])) + ln(0, 0)
    ffn0 = ffn_w(0) + ln(0, 1)
    x2_main = _ffn(x_prompt.reshape(bsz * seq, D_MODEL), *ffn0, pool=(meta,) + pool_args + (seq,))
    x1_meta = _pool_meta(meta, *pool_args)
    xs_sample = jnp.concatenate([jnp.swapaxes(state_pool[0], 0, 1), jnp.swapaxes(x_sample, 0, 1)], axis=0)
    x1_sample = _pool_sample(xs_sample, *pool_args)
    x2_small = _ffn(jnp.concatenate([x1_sample, x1_meta], axis=0), *ffn0)

    cos_main, sin_main = _rope_tables(N_META + jnp.arange(seq))
    pos_small = jnp.concatenate([jnp.full((n_samples,), past_len), jnp.arange(N_META)])
    cos_small, sin_small = _rope_tables(pos_small)
    lq_w = (wkva, vec(kv_norm), wqa, vec(q_norm[0]), wqb, wk)
    ckv_main, kpe_main, klat_main, kpeb_main, qlat_main, qpe_main = _latent_query(
        x2_main, cos_main, sin_main, *lq_w)
    ckv_small, kpe_small, klat_small, kpeb_small, qlat_small, qpe_small = _latent_query(
        x2_small, cos_small, sin_small, *lq_w)

    pad_keys = lambda k: jnp.pad(k[n_samples:], ((0, LANES - N_META), (0, 0)))
    a_main = _flash(qlat_main, qpe_main, klat_main, kpeb_main,
                    pad_keys(klat_small), pad_keys(kpeb_small), wv, bsz)
    o_sample = _decode_attention(
        page_table,
        jnp.swapaxes(qlat_small[:, :n_samples], 0, 1), jnp.swapaxes(qpe_small[:, :n_samples], 0, 1),
        klat_small[:n_samples, None, :], kpeb_small[:n_samples, None, :],
        cache_ckv, jnp.swapaxes(cache_kpe, 1, 2))
    a_sample = _value_proj(jnp.swapaxes(o_sample, 0, 1), wv)

    ffn1 = ffn_w(1) + ln(1, 1)
    y_main = _ffn(x2_main, *ffn1, proj=(a_main, wob) + ln(1, 0))
    y_sample = _ffn(x2_small[:n_samples], *ffn1, proj=(a_sample, wob) + ln(1, 0))

    pool_prompt = x_prompt[None, :, seq - POOL_HIST:, :]
    pool_sample = jnp.concatenate([state_pool[0][:, 1:], x_sample], axis=1)[None]
    bcast = lambda t: jnp.broadcast_to(t[n_samples:][None], (bsz, N_META, t.shape[1]))
    ckv_prompt = jnp.concatenate([bcast(ckv_small), ckv_main.reshape(bsz, seq, KV_LORA)], axis=1)
    kpe_prompt = jnp.concatenate([bcast(kpe_small), kpe_main.reshape(bsz, seq, QK_ROPE)], axis=1)
    return (y_main.reshape(bsz, seq, D_MODEL), y_sample.reshape(n_samples, 1, D_MODEL),
            pool_prompt, pool_sample, ckv_prompt, kpe_prompt,
            ckv_small[:n_samples, None, :], kpe_small[:n_samples, None, :])
```

```python
import functools

import jax
import jax.numpy as jnp
from jax import lax
from jax.experimental import pallas as pl
from jax.experimental.pallas import tpu as pltpu

F32 = jnp.float32
BF16 = jnp.bfloat16

D_MODEL = 1024
DEPTH = 2
N_META = 16
POOL_WINDOWS = (2, 4, 8, 16)
POOL_GROUP = D_MODEL // len(POOL_WINDOWS)
POOL_HIST = max(POOL_WINDOWS) - 1
N_HEADS = 8
QK_NOPE = 128
QK_ROPE = 64
V_HEAD = 128
KV_LORA = 256
ROPE_THETA = 10000.0
ALPHA = (2 * DEPTH) ** 0.25
LN_EPS = 1e-5
RMS_EPS = 1e-6
SM_SCALE = (QK_NOPE + QK_ROPE) ** -0.5
PAGE_SIZE = 128

LANES = 128
MASK_VALUE = -1e30
VMEM_LIMIT = 48 * 1024 * 1024

ROW_TILE = 512
FFN_UNIT_ROWS = 256
ATTN_TILE = 512
FLASH_UNIT_ROWS = 256
FLASH_KEY_ROWS = 256
FLASH_MAX_LAG = 1
FLASH_EXP_LAG = 2
FLASH_VALUE_LAG = 3
LOG2_E = 1.4426950408889634
PAGES_PER_CHUNK = 32
DECODE_SLOTS = 3


def _dot(a, b):
    return jnp.dot(a, b, preferred_element_type=F32)


def _dot_nt(a, b):
    return lax.dot_general(a, b, (((1,), (1,)), ((), ())), preferred_element_type=F32)


def _layer_norm(x, g, b):
    mu = jnp.mean(x, axis=-1, keepdims=True)
    xc = x - mu
    var = jnp.mean(xc * xc, axis=-1, keepdims=True)
    return xc * lax.rsqrt(var + LN_EPS) * g + b


def _rms_norm(x, g):
    return x * lax.rsqrt(jnp.mean(x * x, axis=-1, keepdims=True) + RMS_EPS) * g


def _stagger(stage_generators):
    live = dict(enumerate(stage_generators))
    t = 0
    while live:
        for k in sorted(live):
            if t >= k:
                try:
                    next(live[k])
                except StopIteration:
                    del live[k]
        t += 1


def _const_spec(shape):
    zeros = (0,) * len(shape)
    return pl.BlockSpec(shape, lambda *_: zeros, pipeline_mode=pl.Buffered(1))


def _params(n_axes):
    return pltpu.CompilerParams(dimension_semantics=("arbitrary",) * n_axes,
                                vmem_limit_bytes=VMEM_LIMIT)


def _window_sums(xs, n_rows, pad):
    hi = xs.astype(BF16)
    lo = (xs - hi.astype(F32)).astype(BF16)
    k = xs.shape[0]
    diff = (lax.broadcasted_iota(jnp.int32, (n_rows, k), 0) + pad
            - lax.broadcasted_iota(jnp.int32, (n_rows, k), 1))
    sums = []
    for g, win in enumerate(POOL_WINDOWS):
        band = jnp.where(diff >= 0, jnp.where(diff < win, 1.0, 0.0), 0.0).astype(BF16)
        cols = slice(g * POOL_GROUP, (g + 1) * POOL_GROUP)
        sums.append(_dot(band, hi[:, cols]) + _dot(band, lo[:, cols]))
    return sums


def _pool_finish(x, means, w_ref, scale, g, b):
    outs = []
    for gi in range(len(POOL_WINDOWS)):
        cols = slice(gi * POOL_GROUP, (gi + 1) * POOL_GROUP)
        d = means[gi] - x[:, cols]
        outs.append(_dot(d.astype(BF16), w_ref[gi]))
    h = jnp.concatenate(outs, axis=1) * scale
    return _layer_norm(ALPHA * x + h, g, b)


def _blocked_window_means(xs, n_rows):
    hi = xs.astype(BF16)
    lo = (xs - hi.astype(F32)).astype(BF16)
    diff = (lax.broadcasted_iota(jnp.int32, (LANES, 2 * LANES), 0) + LANES
            - lax.broadcasted_iota(jnp.int32, (LANES, 2 * LANES), 1))
    means = []
    for g, win in enumerate(POOL_WINDOWS):
        band = jnp.where(diff >= 0, jnp.where(diff < win, 1.0, 0.0), 0.0).astype(BF16)
        cols = slice(g * POOL_GROUP, (g + 1) * POOL_GROUP)
        blocks = []
        for r in range(0, n_rows, LANES):
            rows = slice(r, r + 2 * LANES)
            blocks.append(_dot(band, hi[rows, cols]) + _dot(band, lo[rows, cols]))
        means.append(jnp.concatenate(blocks, axis=0) * (1.0 / win))
    return means


def _pool_main_rows(x, halo, w_ref, scale, g, b):
    xs = jnp.concatenate([jnp.zeros((LANES - N_META, D_MODEL), F32), halo, x], axis=0)
    return _pool_finish(x, _blocked_window_means(xs, x.shape[0]), w_ref, scale, g, b)


def _pool_meta_kernel(x_ref, w_ref, scale_ref, g_ref, b_ref, o_ref):
    x = x_ref[...]
    n = x.shape[0]
    xs = jnp.concatenate([jnp.zeros((LANES - n, D_MODEL), F32), x], axis=0)
    sums = _window_sums(xs, n, LANES - n)
    row = lax.broadcasted_iota(jnp.int32, (n, 1), 0)
    means = [s / jnp.minimum(row + 1, w).astype(F32) for s, w in zip(sums, POOL_WINDOWS)]
    o_ref[...] = _pool_finish(x, means, w_ref, scale_ref[...], g_ref[...], b_ref[...])


def _pool_sample_kernel(xs_ref, w_ref, scale_ref, g_ref, b_ref, o_ref):
    n_ctx = xs_ref.shape[0]
    x = xs_ref[n_ctx - 1]
    means = []
    for gi, win in enumerate(POOL_WINDOWS):
        cols = slice(gi * POOL_GROUP, (gi + 1) * POOL_GROUP)
        s = xs_ref[n_ctx - win, :, cols]
        for k in range(n_ctx - win + 1, n_ctx):
            s = s + xs_ref[k, :, cols]
        means.append(s * (1.0 / win))
    o_ref[...] = _pool_finish(x, means, w_ref, scale_ref[...], g_ref[...], b_ref[...])


def _pool_weight_specs():
    return [_const_spec((len(POOL_WINDOWS), POOL_GROUP, POOL_GROUP)),
            _const_spec((1, D_MODEL)), _const_spec((1, D_MODEL)), _const_spec((1, D_MODEL))]


def _pool_meta(meta, pool_w, scale, g, b):
    return pl.pallas_call(
        _pool_meta_kernel,
        grid=(1,),
        in_specs=[_const_spec(meta.shape)] + _pool_weight_specs(),
        out_specs=_const_spec(meta.shape),
        out_shape=jax.ShapeDtypeStruct(meta.shape, F32),
        compiler_params=_params(1),
        name="pool_meta",
    )(meta, pool_w, scale, g, b)


def _pool_sample(xs, pool_w, scale, g, b):
    rows = xs.shape[1]
    return pl.pallas_call(
        _pool_sample_kernel,
        grid=(1,),
        in_specs=[_const_spec(xs.shape)] + _pool_weight_specs(),
        out_specs=_const_spec((rows, D_MODEL)),
        out_shape=jax.ShapeDtypeStruct((rows, D_MODEL), F32),
        compiler_params=_params(1),
        name="pool_sample",
    )(xs, pool_w, scale, g, b)


def _ffn_kernel(*refs, mixer, n_chunks, tiles_per_seq, unit_rows):
    if mixer == "proj":
        a_ref, x_ref, wo_ref, g1_ref, b1_ref = refs[:5]
        refs = refs[5:]
    elif mixer == "pool":
        x_ref, prev_ref, meta_ref, pw_ref, scale_ref, g1_ref, b1_ref = refs[:7]
        refs = refs[7:]
    else:
        x_ref = refs[0]
        refs = refs[1:]
    wg_ref, wu_ref, wd_ref, g2_ref, b2_ref, o_ref = refs
    chunk = wg_ref.shape[1] // n_chunks

    def unit_stages(r0):
        rows = slice(r0, r0 + unit_rows)
        if mixer == "proj":
            x = _layer_norm(ALPHA * x_ref[rows, :] + _dot(a_ref[rows, :], wo_ref[...]), g1_ref[...], b1_ref[...])
        elif mixer == "pool":
            if r0 == 0:
                halo = jnp.where(pl.program_id(0) % tiles_per_seq == 0, meta_ref[...], prev_ref[...])
            else:
                halo = x_ref[r0 - N_META:r0, :]
            x = _pool_main_rows(x_ref[rows, :], halo, pw_ref, scale_ref[...], g1_ref[...], b1_ref[...])
        else:
            x = x_ref[rows, :]
        xb = x.astype(BF16)
        yield
        acc = None
        for c in range(n_chunks):
            cols = slice(c * chunk, (c + 1) * chunk)
            gate = _dot(xb, wg_ref[:, cols])
            up = _dot(xb, wu_ref[:, cols])
            mid = (gate * (1.0 / (1.0 + jnp.exp(-gate))) * up).astype(BF16)
            part = _dot(mid, wd_ref[cols, :])
            acc = part if acc is None else acc + part
            yield
        o_ref[rows, :] = _layer_norm(ALPHA * x + acc, g2_ref[...], b2_ref[...])
        yield

    _stagger([unit_stages(r0) for r0 in range(0, x_ref.shape[0], unit_rows)])


def _ffn(x, wg, wu, wd, layer, g2, b2, proj=None, pool=None):
    rows = x.shape[0]
    layer_spec = lambda w: pl.BlockSpec((None,) + w.shape[1:], lambda *_: (layer, 0, 0),
                                        pipeline_mode=pl.Buffered(1))
    tile = min(ROW_TILE, rows)
    row_spec = pl.BlockSpec((tile, D_MODEL), lambda i: (i, 0))
    vec_spec = _const_spec((1, D_MODEL))
    args, specs = [], []
    mixer, tiles_per_seq = None, None
    if proj is not None:
        mixer = "proj"
        a, wo, g1, b1 = proj
        args += [a, x, wo, g1, b1]
        specs += [row_spec, row_spec, _const_spec(wo.shape), vec_spec, vec_spec]
    elif pool is not None:
        mixer = "pool"
        meta, pool_w, scale, g1, b1, seq = pool
        assert seq % tile == 0 and tile % LANES == 0
        tiles_per_seq = seq // tile
        halo_blocks = tile // N_META
        args += [x, x, meta, pool_w, scale, g1, b1]
        specs += [row_spec,
                  pl.BlockSpec((N_META, D_MODEL), lambda i: (jnp.maximum(i * halo_blocks - 1, 0), 0)),
                  _const_spec(meta.shape)] + _pool_weight_specs()
    else:
        args += [x]
        specs += [row_spec]
    args += [wg, wu, wd, g2, b2]
    specs += [layer_spec(wg), layer_spec(wu), layer_spec(wd), vec_spec, vec_spec]
    return pl.pallas_call(
        functools.partial(_ffn_kernel, mixer=mixer, n_chunks=2, tiles_per_seq=tiles_per_seq,
                          unit_rows=FFN_UNIT_ROWS if tile % FFN_UNIT_ROWS == 0 else tile),
        grid=(rows // tile,),
        in_specs=specs,
        out_specs=row_spec,
        out_shape=jax.ShapeDtypeStruct((rows, D_MODEL), F32),
        compiler_params=_params(1),
        name="ffn" if mixer is None else "ffn_" + mixer,
    )(*args)


def _latent_query_kernel(x_ref, cos_ref, sin_ref, wkva_ref, kvn_ref, wqa_ref, qn_ref, wqb_ref, wk_ref,
                         ckv_ref, kpe_ref, klat_ref, kpeb_ref, qlat_ref, qpe_ref):
    xb = x_ref[...].astype(BF16)
    cos = cos_ref[...]
    sin = sin_ref[...]
    kv = _dot(xb, wkva_ref[...])
    ckv = _rms_norm(kv[:, :KV_LORA], kvn_ref[...])
    kpe = kv[:, KV_LORA:KV_LORA + QK_ROPE] * cos + kv[:, KV_LORA + QK_ROPE:] * sin
    ckv_ref[...] = ckv
    kpe_ref[...] = kpe
    klat_ref[...] = ckv.astype(BF16)
    kpeb_ref[...] = kpe.astype(BF16)
    qa = _rms_norm(_dot(xb, wqa_ref[...]), qn_ref[...]).astype(BF16)
    q = _dot(qa, wqb_ref[...])
    per_head = QK_NOPE + 2 * QK_ROPE
    for h in range(N_HEADS):
        qh = q[:, h * per_head:(h + 1) * per_head]
        qlat = _dot(qh[:, :QK_NOPE].astype(BF16), wk_ref[h]) * (SM_SCALE * LOG2_E)
        qpe = (qh[:, QK_NOPE:QK_NOPE + QK_ROPE] * cos + qh[:, QK_NOPE + QK_ROPE:] * sin) * (SM_SCALE * LOG2_E)
        qlat_ref[h] = qlat.astype(BF16)
        qpe_ref[h] = qpe.astype(BF16)


def _latent_query(x, cos, sin, wkva, kvn, wqa, qn, wqb, wk):
    rows = x.shape[0]
    tile = min(ROW_TILE, rows)
    pos_tiles = cos.shape[0] // tile
    row = lambda width: pl.BlockSpec((tile, width), lambda i: (i, 0))
    pos = pl.BlockSpec((tile, QK_ROPE), lambda i: (i % pos_tiles, 0))
    head = lambda width: pl.BlockSpec((N_HEADS, tile, width), lambda i: (0, i, 0))
    return pl.pallas_call(
        _latent_query_kernel,
        grid=(rows // tile,),
        in_specs=[row(D_MODEL), pos, pos, _const_spec(wkva.shape), _const_spec(kvn.shape),
                  _const_spec(wqa.shape), _const_spec(qn.shape), _const_spec(wqb.shape),
                  _const_spec(wk.shape)],
        out_specs=[row(KV_LORA), row(QK_ROPE), row(KV_LORA), row(QK_ROPE), head(KV_LORA), head(QK_ROPE)],
        out_shape=[jax.ShapeDtypeStruct((rows, KV_LORA), F32),
                   jax.ShapeDtypeStruct((rows, QK_ROPE), F32),
                   jax.ShapeDtypeStruct((rows, KV_LORA), BF16),
                   jax.ShapeDtypeStruct((rows, QK_ROPE), BF16),
                   jax.ShapeDtypeStruct((N_HEADS, rows, KV_LORA), BF16),
                   jax.ShapeDtypeStruct((N_HEADS, rows, QK_ROPE), BF16)],
        compiler_params=_params(1),
        name="latent_query",
    )(x, cos, sin, wkva, kvn, wqa, qn, wqb, wk)


def _lane_tile(v, width):
    return jnp.concatenate([v] * (width // LANES), axis=1)


def _flash_kernel(ql_ref, qp_ref, kl_ref, kp_ref, ml_ref, mp_ref, wv_ref, o_ref, m_sc, l_sc, acc_sc):
    tq = ql_ref.shape[1]
    i = pl.program_id(1)
    units = [(h, pl.ds(r, FLASH_UNIT_ROWS)) for h in range(N_HEADS) for r in range(0, tq, FLASH_UNIT_ROWS)]

    def scores(u, keys):
        h, rows = units[u]
        return [_dot_nt(ql_ref[h, rows, :], kl) + _dot_nt(qp_ref[h, rows, :], kp) for kl, kp in keys]

    def max_part(u, s_list, first):
        h, rows = units[u]
        m_cur = jnp.max(s_list[0], axis=1, keepdims=True)
        for s in s_list[1:]:
            m_cur = jnp.maximum(m_cur, jnp.max(s, axis=1, keepdims=True))
        if first:
            return jnp.broadcast_to(m_cur, (FLASH_UNIT_ROWS, LANES)), None
        m_prev = m_sc[h, rows, :]
        m_new = jnp.maximum(m_prev, m_cur)
        return m_new, jnp.exp2(m_prev - m_new)

    def exp_part(u, s_list, m_new, alpha, first):
        h, rows = units[u]
        p_list = [jnp.exp2(s - _lane_tile(m_new, s.shape[1])) for s in s_list]
        p_sum = None
        for p in p_list:
            for c in range(p.shape[1] // LANES):
                blk = p[:, c * LANES:(c + 1) * LANES]
                p_sum = blk if p_sum is None else p_sum + blk
        l_sc[h, rows, :] = p_sum if first else alpha * l_sc[h, rows, :] + p_sum
        m_sc[h, rows, :] = m_new
        return [p.astype(BF16) for p in p_list]

    def value_part(u, p_list, alpha, keys, first):
        h, rows = units[u]
        pv = None
        for p, (kl, _) in zip(p_list, keys):
            d = _dot(p, kl)
            pv = d if pv is None else pv + d
        acc_sc[h, rows, :] = pv if first else _lane_tile(alpha, KV_LORA) * acc_sc[h, rows, :] + pv

    def step(unit_keys, first):
        s_q, m_q, p_q = {}, {}, {}
        for t in range(len(units) + FLASH_VALUE_LAG):
            if t < len(units):
                s_q[t] = scores(t, unit_keys(t)[0])
            u = t - FLASH_MAX_LAG
            if 0 <= u < len(units):
                s_q[u] = [s if mk is None else jnp.where(mk(s.shape), s, MASK_VALUE)
                          for s, mk in zip(s_q[u], unit_keys(u)[1])]
                m_q[u] = max_part(u, s_q[u], first)
            u = t - FLASH_EXP_LAG
            if 0 <= u < len(units):
                p_q[u] = exp_part(u, s_q.pop(u), *m_q[u], first)
            u = t - FLASH_VALUE_LAG
            if 0 <= u < len(units):
                value_part(u, p_q.pop(u), m_q.pop(u)[1], unit_keys(u)[0], first)

    def key_tile(start, size):
        off = pl.multiple_of(start, FLASH_UNIT_ROWS)
        return kl_ref[pl.ds(off, size), :], kp_ref[pl.ds(off, size), :]

    meta_mask = lambda shape: lax.broadcasted_iota(jnp.int32, shape, 1) < N_META
    causal_mask = lambda shape: (lax.broadcasted_iota(jnp.int32, shape, 1)
                                 <= lax.broadcasted_iota(jnp.int32, shape, 0))
    meta_keys = (ml_ref[...], mp_ref[...])
    diag_tiles = {r: key_tile(i * tq + r, FLASH_UNIT_ROWS) for r in range(0, tq, FLASH_UNIT_ROWS)}

    def first_keys(u):
        row0 = units[u][1].start
        below = list(range(0, row0, FLASH_UNIT_ROWS))
        return ([meta_keys] + [diag_tiles[r] for r in below] + [diag_tiles[row0]],
                [meta_mask] + [None] * len(below) + [causal_mask])

    step(first_keys, True)

    def body(j, carry):
        full_tile = key_tile(j * FLASH_KEY_ROWS, FLASH_KEY_ROWS)
        step(lambda u: ([full_tile], [None]), False)
        return carry

    lax.fori_loop(0, i * (tq // FLASH_KEY_ROWS), body, 0)

    for h in range(N_HEADS):
        l_row = jnp.sum(l_sc[h], axis=1, keepdims=True)
        o = (acc_sc[h] / l_row).astype(BF16)
        o_ref[:, h * V_HEAD:(h + 1) * V_HEAD] = _dot(o, wv_ref[h]).astype(BF16)


def _flash(qlat, qpe, klat, kpe, mlat, mpe, wv, bsz):
    rows = klat.shape[0]
    seq = rows // bsz
    tq = ATTN_TILE
    assert seq % tq == 0 and tq % FLASH_UNIT_ROWS == 0
    nq = seq // tq
    return pl.pallas_call(
        _flash_kernel,
        grid=(bsz, nq),
        in_specs=[pl.BlockSpec((N_HEADS, tq, KV_LORA), lambda b, i: (0, b * nq + i, 0)),
                  pl.BlockSpec((N_HEADS, tq, QK_ROPE), lambda b, i: (0, b * nq + i, 0)),
                  pl.BlockSpec((seq, KV_LORA), lambda b, i: (b, 0)),
                  pl.BlockSpec((seq, QK_ROPE), lambda b, i: (b, 0)),
                  _const_spec(mlat.shape), _const_spec(mpe.shape), _const_spec(wv.shape)],
        out_specs=pl.BlockSpec((tq, N_HEADS * V_HEAD), lambda b, i: (b * nq + i, 0)),
        out_shape=jax.ShapeDtypeStruct((rows, N_HEADS * V_HEAD), BF16),
        scratch_shapes=[pltpu.VMEM((N_HEADS, tq, LANES), F32), pltpu.VMEM((N_HEADS, tq, LANES), F32),
                        pltpu.VMEM((N_HEADS, tq, KV_LORA), F32)],
        compiler_params=_params(2),
        name="flash_prompt",
    )(qlat, qpe, klat, kpe, mlat, mpe, wv)


def _decode_kernel(pt_ref, ql_ref, qp_ref, kn_ref, pn_ref, ckv_hbm, kpe_hbm, o_ref, kbuf, pbuf, sem,
                   *, n_chunks):
    b = pl.program_id(0)
    n_samples = pl.num_programs(0)
    ahead = DECODE_SLOTS - 1

    def chunk_copies(sample, c):
        slot = (sample * n_chunks + c) % DECODE_SLOTS
        copies = []
        for p in range(PAGES_PER_CHUNK):
            page = pt_ref[sample, c * PAGES_PER_CHUNK + p]
            dst = pl.ds(p * PAGE_SIZE, PAGE_SIZE)
            copies.append(pltpu.make_async_copy(ckv_hbm.at[page], kbuf.at[slot, dst, :], sem.at[0, slot]))
            copies.append(pltpu.make_async_copy(kpe_hbm.at[page], pbuf.at[slot, :, dst], sem.at[1, slot]))
        return slot, copies

    @pl.when(b == 0)
    def _():
        for c in range(ahead):
            for cp in chunk_copies(0, c)[1]:
                cp.start()

    ql = ql_ref[0]
    qp = qp_ref[0]
    kn = kn_ref[0].astype(F32)
    pn = pn_ref[0].astype(F32)
    m_run = (jnp.sum(ql.astype(F32) * kn, axis=1, keepdims=True)
             + jnp.sum(qp.astype(F32) * pn, axis=1, keepdims=True))
    l_run = jnp.ones_like(m_run)
    acc = jnp.broadcast_to(kn, (N_HEADS, KV_LORA))

    for c in range(n_chunks):
        nxt_sample, nxt_c = b + (c + ahead) // n_chunks, (c + ahead) % n_chunks
        if c + ahead < n_chunks:
            for cp in chunk_copies(nxt_sample, nxt_c)[1]:
                cp.start()
        else:
            @pl.when(nxt_sample < n_samples)
            def _():
                for cp in chunk_copies(nxt_sample, nxt_c)[1]:
                    cp.start()
        slot, copies = chunk_copies(b, c)
        for cp in copies:
            cp.wait()
        k = kbuf[slot].astype(BF16)
        pe = pbuf[slot].astype(BF16)
        s = _dot_nt(ql, k) + _dot(qp, pe)
        m_new = jnp.maximum(m_run, jnp.max(s, axis=1, keepdims=True))
        alpha = jnp.exp2(m_run - m_new)
        p = jnp.exp2(s - m_new)
        l_run = alpha * l_run + jnp.sum(p, axis=1, keepdims=True)
        acc = alpha * acc + _dot(p.astype(BF16), k)
        m_run = m_new
    o_ref[0] = acc / l_run


def _decode_attention(page_table, qlat, qpe, knew, pnew, cache_ckv, cache_kpe):
    n_samples, n_pages = page_table.shape
    chunk_rows = PAGES_PER_CHUNK * PAGE_SIZE
    n_chunks = n_pages // PAGES_PER_CHUNK
    assert n_pages % PAGES_PER_CHUNK == 0 and n_chunks >= DECODE_SLOTS - 1
    per_sample = lambda width, lead: pl.BlockSpec((1, lead, width), lambda b, pt: (b, 0, 0))
    grid_spec = pltpu.PrefetchScalarGridSpec(
        num_scalar_prefetch=1,
        grid=(n_samples,),
        in_specs=[per_sample(KV_LORA, N_HEADS), per_sample(QK_ROPE, N_HEADS),
                  per_sample(KV_LORA, 1), per_sample(QK_ROPE, 1),
                  pl.BlockSpec(memory_space=pl.ANY), pl.BlockSpec(memory_space=pl.ANY)],
        out_specs=per_sample(KV_LORA, N_HEADS),
        scratch_shapes=[pltpu.VMEM((DECODE_SLOTS, chunk_rows, KV_LORA), F32),
                        pltpu.VMEM((DECODE_SLOTS, QK_ROPE, chunk_rows), F32),
                        pltpu.SemaphoreType.DMA((2, DECODE_SLOTS))],
    )
    return pl.pallas_call(
        functools.partial(_decode_kernel, n_chunks=n_chunks),
        grid_spec=grid_spec,
        out_shape=jax.ShapeDtypeStruct((n_samples, N_HEADS, KV_LORA), F32),
        compiler_params=_params(1),
        name="decode_attention",
    )(page_table, qlat, qpe, knew, pnew, cache_ckv, cache_kpe)


def _value_proj_kernel(o_ref, wv_ref, a_ref):
    for h in range(N_HEADS):
        a_ref[:, h * V_HEAD:(h + 1) * V_HEAD] = _dot(o_ref[h].astype(BF16), wv_ref[h]).astype(BF16)


def _value_proj(o, wv):
    rows = o.shape[1]
    return pl.pallas_call(
        _value_proj_kernel,
        grid=(1,),
        in_specs=[_const_spec(o.shape), _const_spec(wv.shape)],
        out_specs=_const_spec((rows, N_HEADS * V_HEAD)),
        out_shape=jax.ShapeDtypeStruct((rows, N_HEADS * V_HEAD), BF16),
        compiler_params=_params(1),
        name="value_proj",
    )(o, wv)


def _rope_tables(pos):
    half = QK_ROPE // 2
    inv = ROPE_THETA ** (-jnp.arange(half, dtype=F32) / half)
    ang = pos.astype(F32)[:, None] * inv[None, :]
    cos, sin = jnp.cos(ang), jnp.sin(ang)
    return jnp.concatenate([cos, cos], axis=1), jnp.concatenate([-sin, sin], axis=1)


def _rotate_half_cols(w):
    half = QK_ROPE // 2
    return jnp.concatenate([w[..., half:], w[..., :half]], axis=-1)


def kernel(x_prompt, x_sample, state_pool, cache_ckv, cache_kpe, page_table, meta_tokens,
           ln_g, ln_b, pool_w, pool_scale, w_gate, w_up, w_down,
           wq_a, q_norm, wq_b, wo, wkv_a, kv_norm, wkv_b):
    bsz, seq, _ = x_prompt.shape
    n_samples = x_sample.shape[0]
    past_len = page_table.shape[1] * PAGE_SIZE

    vec = lambda v: v.reshape(1, -1).astype(F32)
    pool_wb = pool_w[0].astype(BF16)
    wg, wu, wd = w_gate.astype(BF16), w_up.astype(BF16), w_down.astype(BF16)
    ffn_w = lambda l: (wg, wu, wd, l)
    wkva = jnp.concatenate([wkv_a, _rotate_half_cols(wkv_a[:, KV_LORA:])], axis=1).astype(BF16)
    wqb3 = wq_b[0].reshape(wq_b.shape[1], N_HEADS, QK_NOPE + QK_ROPE)
    wqb = jnp.concatenate([wqb3, _rotate_half_cols(wqb3[..., QK_NOPE:])], axis=-1)
    wqb = wqb.reshape(wq_b.shape[1], -1).astype(BF16)
    wk = jnp.transpose(wkv_b[:, :, :QK_NOPE], (1, 2, 0)).astype(BF16)
    wv = jnp.transpose(wkv_b[:, :, QK_NOPE:], (1, 0, 2)).astype(BF16)
    wqa, wob = wq_a[0].astype(BF16), wo[0].astype(BF16)
    ln = lambda l, k: (vec(ln_g[l, k]), vec(ln_b[l, k]))

    meta = meta_tokens.astype(F32)
    pool_args = (pool_wb, vec(pool_scale[0])) + ln(0, 0)
    ffn0 = ffn_w(0) + ln(0, 1)
    x2_main = _ffn(x_prompt.reshape(bsz * seq, D_MODEL), *ffn0, pool=(meta,) + pool_args + (seq,))
    x1_meta = _pool_meta(meta, *pool_args)
    xs_sample = jnp.concatenate([jnp.swapaxes(state_pool[0], 0, 1), jnp.swapaxes(x_sample, 0, 1)], axis=0)
    x1_sample = _pool_sample(xs_sample, *pool_args)
    x2_small = _ffn(jnp.concatenate([x1_sample, x1_meta], axis=0), *ffn0)

    cos_main, sin_main = _rope_tables(N_META + jnp.arange(seq))
    pos_small = jnp.concatenate([jnp.full((n_samples,), past_len), jnp.arange(N_META)])
    cos_small, sin_small = _rope_tables(pos_small)
    lq_w = (wkva, vec(kv_norm), wqa, vec(q_norm[0]), wqb, wk)
    ckv_main, kpe_main, klat_main, kpeb_main, qlat_main, qpe_main = _latent_query(
        x2_main, cos_main, sin_main, *lq_w)
    ckv_small, kpe_small, klat_small, kpeb_small, qlat_small, qpe_small = _latent_query(
        x2_small, cos_small, sin_small, *lq_w)

    pad_keys = lambda k: jnp.pad(k[n_samples:], ((0, LANES - N_META), (0, 0)))
    a_main = _flash(qlat_main, qpe_main, klat_main, kpeb_main,
                    pad_keys(klat_small), pad_keys(kpeb_small), wv, bsz)
    o_sample = _decode_attention(
        page_table,
        jnp.swapaxes(qlat_small[:, :n_samples], 0, 1), jnp.swapaxes(qpe_small[:, :n_samples], 0, 1),
        klat_small[:n_samples, None, :], kpeb_small[:n_samples, None, :],
        cache_ckv, jnp.swapaxes(cache_kpe, 1, 2))
    a_sample = _value_proj(jnp.swapaxes(o_sample, 0, 1), wv)

    ffn1 = ffn_w(1) + ln(1, 1)
    y_main = _ffn(x2_main, *ffn1, proj=(a_main, wob) + ln(1, 0))
    y_sample = _ffn(x2_small[:n_samples], *ffn1, proj=(a_sample, wob) + ln(1, 0))

    pool_prompt = x_prompt[None, :, seq - POOL_HIST:, :]
    pool_sample = jnp.concatenate([state_pool[0][:, 1:], x_sample], axis=1)[None]
    bcast = lambda t: jnp.broadcast_to(t[n_samples:][None], (bsz, N_META, t.shape[1]))
    ckv_prompt = jnp.concatenate([bcast(ckv_small), ckv_main.reshape(bsz, seq, KV_LORA)], axis=1)
    kpe_prompt = jnp.concatenate([bcast(kpe_small), kpe_main.reshape(bsz, seq, QK_ROPE)], axis=1)
    return (y_main.reshape(bsz, seq, D_MODEL), y_sample.reshape(n_samples, 1, D_MODEL),
            pool_prompt, pool_sample, ckv_prompt, kpe_prompt,
            ckv_small[:n_samples, None, :], kpe_small[:n_samples, None, :])
```

```python
import functools

import jax
import jax.numpy as jnp
from jax import lax
from jax.experimental import pallas as pl
from jax.experimental.pallas import tpu as pltpu

F32 = jnp.float32
BF16 = jnp.bfloat16

D_MODEL = 1024
DEPTH = 2
N_META = 16
POOL_WINDOWS = (2, 4, 8, 16)
POOL_GROUP = D_MODEL // len(POOL_WINDOWS)
POOL_HIST = max(POOL_WINDOWS) - 1
N_HEADS = 8
QK_NOPE = 128
QK_ROPE = 64
V_HEAD = 128
KV_LORA = 256
ROPE_THETA = 10000.0
ALPHA = (2 * DEPTH) ** 0.25
LN_EPS = 1e-5
RMS_EPS = 1e-6
SM_SCALE = (QK_NOPE + QK_ROPE) ** -0.5
PAGE_SIZE = 128

LANES = 128
MASK_VALUE = -1e30
VMEM_LIMIT = 48 * 1024 * 1024

ROW_TILE = 512
FFN_ROW_TILE = 1024
FFN_UNIT_ROWS = 512
FFN_VMEM_LIMIT = 58 * 1024 * 1024
ATTN_TILE = 1024
FLASH_UNIT_ROWS = 256
FLASH_KEY_ROWS = 512
FLASH_MAX_LAG = 1
FLASH_EXP_LAG = 2
FLASH_VALUE_LAG = 3
LOG2_E = 1.4426950408889634
PAGES_PER_CHUNK = 32
DECODE_SLOTS = 3


def _dot(a, b):
    return jnp.dot(a, b, preferred_element_type=F32)


def _dot_nt(a, b):
    return lax.dot_general(a, b, (((1,), (1,)), ((), ())), preferred_element_type=F32)


def _layer_norm(x, g, b):
    mu = jnp.mean(x, axis=-1, keepdims=True)
    xc = x - mu
    var = jnp.mean(xc * xc, axis=-1, keepdims=True)
    return xc * lax.rsqrt(var + LN_EPS) * g + b


def _rms_norm(x, g):
    return x * lax.rsqrt(jnp.mean(x * x, axis=-1, keepdims=True) + RMS_EPS) * g


def _stagger(stage_generators):
    live = dict(enumerate(stage_generators))
    t = 0
    while live:
        for k in sorted(live):
            if t >= k:
                try:
                    next(live[k])
                except StopIteration:
                    del live[k]
        t += 1


def _const_spec(shape):
    zeros = (0,) * len(shape)
    return pl.BlockSpec(shape, lambda *_: zeros, pipeline_mode=pl.Buffered(1))


def _params(n_axes):
    return pltpu.CompilerParams(dimension_semantics=("arbitrary",) * n_axes,
                                vmem_limit_bytes=VMEM_LIMIT)


def _window_sums(xs, n_rows, pad):
    hi = xs.astype(BF16)
    lo = (xs - hi.astype(F32)).astype(BF16)
    k = xs.shape[0]
    diff = (lax.broadcasted_iota(jnp.int32, (n_rows, k), 0) + pad
            - lax.broadcasted_iota(jnp.int32, (n_rows, k), 1))
    sums = []
    for g, win in enumerate(POOL_WINDOWS):
        band = jnp.where(diff >= 0, jnp.where(diff < win, 1.0, 0.0), 0.0).astype(BF16)
        cols = slice(g * POOL_GROUP, (g + 1) * POOL_GROUP)
        sums.append(_dot(band, hi[:, cols]) + _dot(band, lo[:, cols]))
    return sums


def _pool_finish(x, means, w_ref, scale, g, b):
    outs = []
    for gi in range(len(POOL_WINDOWS)):
        cols = slice(gi * POOL_GROUP, (gi + 1) * POOL_GROUP)
        d = means[gi] - x[:, cols]
        outs.append(_dot(d.astype(BF16), w_ref[gi]))
    h = jnp.concatenate(outs, axis=1) * scale
    return _layer_norm(ALPHA * x + h, g, b)


def _blocked_window_means(xs, n_rows):
    hi = xs.astype(BF16)
    lo = (xs - hi.astype(F32)).astype(BF16)
    diff = (lax.broadcasted_iota(jnp.int32, (LANES, 2 * LANES), 0) + LANES
            - lax.broadcasted_iota(jnp.int32, (LANES, 2 * LANES), 1))
    means = []
    for g, win in enumerate(POOL_WINDOWS):
        band = jnp.where(diff >= 0, jnp.where(diff < win, 1.0, 0.0), 0.0).astype(BF16)
        cols = slice(g * POOL_GROUP, (g + 1) * POOL_GROUP)
        blocks = []
        for r in range(0, n_rows, LANES):
            rows = slice(r, r + 2 * LANES)
            blocks.append(_dot(band, hi[rows, cols]) + _dot(band, lo[rows, cols]))
        means.append(jnp.concatenate(blocks, axis=0) * (1.0 / win))
    return means


def _pool_main_rows(x, halo, w_ref, scale, g, b):
    xs = jnp.concatenate([jnp.zeros((LANES - N_META, D_MODEL), F32), halo, x], axis=0)
    return _pool_finish(x, _blocked_window_means(xs, x.shape[0]), w_ref, scale, g, b)


def _pool_meta_kernel(x_ref, w_ref, scale_ref, g_ref, b_ref, o_ref):
    x = x_ref[...]
    n = x.shape[0]
    xs = jnp.concatenate([jnp.zeros((LANES - n, D_MODEL), F32), x], axis=0)
    sums = _window_sums(xs, n, LANES - n)
    row = lax.broadcasted_iota(jnp.int32, (n, 1), 0)
    means = [s / jnp.minimum(row + 1, w).astype(F32) for s, w in zip(sums, POOL_WINDOWS)]
    o_ref[...] = _pool_finish(x, means, w_ref, scale_ref[...], g_ref[...], b_ref[...])


def _pool_sample_kernel(xs_ref, w_ref, scale_ref, g_ref, b_ref, o_ref):
    n_ctx = xs_ref.shape[0]
    x = xs_ref[n_ctx - 1]
    means = []
    for gi, win in enumerate(POOL_WINDOWS):
        cols = slice(gi * POOL_GROUP, (gi + 1) * POOL_GROUP)
        s = xs_ref[n_ctx - win, :, cols]
        for k in range(n_ctx - win + 1, n_ctx):
            s = s + xs_ref[k, :, cols]
        means.append(s * (1.0 / win))
    o_ref[...] = _pool_finish(x, means, w_ref, scale_ref[...], g_ref[...], b_ref[...])


def _pool_weight_specs():
    return [_const_spec((len(POOL_WINDOWS), POOL_GROUP, POOL_GROUP)),
            _const_spec((1, D_MODEL)), _const_spec((1, D_MODEL)), _const_spec((1, D_MODEL))]


def _pool_meta(meta, pool_w, scale, g, b):
    return pl.pallas_call(
        _pool_meta_kernel,
        grid=(1,),
        in_specs=[_const_spec(meta.shape)] + _pool_weight_specs(),
        out_specs=_const_spec(meta.shape),
        out_shape=jax.ShapeDtypeStruct(meta.shape, F32),
        compiler_params=_params(1),
        name="pool_meta",
    )(meta, pool_w, scale, g, b)


def _pool_sample(xs, pool_w, scale, g, b):
    rows = xs.shape[1]
    return pl.pallas_call(
        _pool_sample_kernel,
        grid=(1,),
        in_specs=[_const_spec(xs.shape)] + _pool_weight_specs(),
        out_specs=_const_spec((rows, D_MODEL)),
        out_shape=jax.ShapeDtypeStruct((rows, D_MODEL), F32),
        compiler_params=_params(1),
        name="pool_sample",
    )(xs, pool_w, scale, g, b)


def _ffn_kernel(*refs, mixer, n_chunks, tiles_per_seq, unit_rows):
    if mixer == "proj":
        a_ref, x_ref, wo_ref, g1_ref, b1_ref = refs[:5]
        refs = refs[5:]
    elif mixer == "pool":
        x_ref, prev_ref, meta_ref, pw_ref, scale_ref, g1_ref, b1_ref = refs[:7]
        refs = refs[7:]
    else:
        x_ref = refs[0]
        refs = refs[1:]
    wg_ref, wu_ref, wd_ref, g2_ref, b2_ref, o_ref = refs
    chunk = wg_ref.shape[1] // n_chunks

    def unit_stages(r0):
        rows = slice(r0, r0 + unit_rows)
        if mixer == "proj":
            x = _layer_norm(ALPHA * x_ref[rows, :] + _dot(a_ref[rows, :], wo_ref[...]), g1_ref[...], b1_ref[...])
        elif mixer == "pool":
            if r0 == 0:
                halo = jnp.where(pl.program_id(0) % tiles_per_seq == 0, meta_ref[...], prev_ref[...])
            else:
                halo = x_ref[r0 - N_META:r0, :]
            x = _pool_main_rows(x_ref[rows, :], halo, pw_ref, scale_ref[...], g1_ref[...], b1_ref[...])
        else:
            x = x_ref[rows, :]
        xb = x.astype(BF16)
        yield
        acc = None
        for c in range(n_chunks):
            cols = slice(c * chunk, (c + 1) * chunk)
            gate = _dot(xb, wg_ref[:, cols])
            up = _dot(xb, wu_ref[:, cols])
            mid = (gate * (1.0 / (1.0 + jnp.exp(-gate))) * up).astype(BF16)
            part = _dot(mid, wd_ref[cols, :])
            acc = part if acc is None else acc + part
            yield
        o_ref[rows, :] = _layer_norm(ALPHA * x + acc, g2_ref[...], b2_ref[...])
        yield

    _stagger([unit_stages(r0) for r0 in range(0, x_ref.shape[0], unit_rows)])


def _ffn(x, wg, wu, wd, layer, g2, b2, proj=None, pool=None):
    rows = x.shape[0]
    layer_spec = lambda w: pl.BlockSpec((None,) + w.shape[1:], lambda *_: (layer, 0, 0),
                                        pipeline_mode=pl.Buffered(1))
    tile = min(FFN_ROW_TILE, rows)
    row_spec = pl.BlockSpec((tile, D_MODEL), lambda i: (i, 0))
    vec_spec = _const_spec((1, D_MODEL))
    args, specs = [], []
    mixer, tiles_per_seq = None, None
    if proj is not None:
        mixer = "proj"
        a, wo, g1, b1 = proj
        args += [a, x, wo, g1, b1]
        specs += [row_spec, row_spec, _const_spec(wo.shape), vec_spec, vec_spec]
    elif pool is not None:
        mixer = "pool"
        meta, pool_w, scale, g1, b1, seq = pool
        assert seq % tile == 0 and tile % LANES == 0
        tiles_per_seq = seq // tile
        halo_blocks = tile // N_META
        args += [x, x, meta, pool_w, scale, g1, b1]
        specs += [row_spec,
                  pl.BlockSpec((N_META, D_MODEL), lambda i: (jnp.maximum(i * halo_blocks - 1, 0), 0)),
                  _const_spec(meta.shape)] + _pool_weight_specs()
    else:
        args += [x]
        specs += [row_spec]
    args += [wg, wu, wd, g2, b2]
    specs += [layer_spec(wg), layer_spec(wu), layer_spec(wd), vec_spec, vec_spec]
    return pl.pallas_call(
        functools.partial(_ffn_kernel, mixer=mixer, n_chunks=2, tiles_per_seq=tiles_per_seq,
                          unit_rows=FFN_UNIT_ROWS if tile % FFN_UNIT_ROWS == 0 else tile),
        grid=(rows // tile,),
        in_specs=specs,
        out_specs=row_spec,
        out_shape=jax.ShapeDtypeStruct((rows, D_MODEL), F32),
        compiler_params=pltpu.CompilerParams(dimension_semantics=("arbitrary",),
                                             vmem_limit_bytes=FFN_VMEM_LIMIT),
        name="ffn" if mixer is None else "ffn_" + mixer,
    )(*args)


def _latent_query_kernel(x_ref, cos_ref, sin_ref, wkva_ref, kvn_ref, wqa_ref, qn_ref, wqb_ref, wk_ref,
                         ckv_ref, kpe_ref, klat_ref, kpeb_ref, qlat_ref, qpe_ref):
    xb = x_ref[...].astype(BF16)
    cos = cos_ref[...]
    sin = sin_ref[...]
    kv = _dot(xb, wkva_ref[...])
    ckv = _rms_norm(kv[:, :KV_LORA], kvn_ref[...])
    kpe = kv[:, KV_LORA:KV_LORA + QK_ROPE] * cos + kv[:, KV_LORA + QK_ROPE:] * sin
    ckv_ref[...] = ckv
    kpe_ref[...] = kpe
    klat_ref[...] = ckv.astype(BF16)
    kpeb_ref[...] = kpe.astype(BF16)
    qa = _rms_norm(_dot(xb, wqa_ref[...]), qn_ref[...]).astype(BF16)
    q = _dot(qa, wqb_ref[...])
    per_head = QK_NOPE + 2 * QK_ROPE
    for h in range(N_HEADS):
        qh = q[:, h * per_head:(h + 1) * per_head]
        qlat = _dot(qh[:, :QK_NOPE].astype(BF16), wk_ref[h]) * (SM_SCALE * LOG2_E)
        qpe = (qh[:, QK_NOPE:QK_NOPE + QK_ROPE] * cos + qh[:, QK_NOPE + QK_ROPE:] * sin) * (SM_SCALE * LOG2_E)
        qlat_ref[h] = qlat.astype(BF16)
        qpe_ref[h] = qpe.astype(BF16)


def _latent_query(x, cos, sin, wkva, kvn, wqa, qn, wqb, wk):
    rows = x.shape[0]
    tile = min(ROW_TILE, rows)
    pos_tiles = cos.shape[0] // tile
    row = lambda width: pl.BlockSpec((tile, width), lambda i: (i, 0))
    pos = pl.BlockSpec((tile, QK_ROPE), lambda i: (i % pos_tiles, 0))
    head = lambda width: pl.BlockSpec((N_HEADS, tile, width), lambda i: (0, i, 0))
    return pl.pallas_call(
        _latent_query_kernel,
        grid=(rows // tile,),
        in_specs=[row(D_MODEL), pos, pos, _const_spec(wkva.shape), _const_spec(kvn.shape),
                  _const_spec(wqa.shape), _const_spec(qn.shape), _const_spec(wqb.shape),
                  _const_spec(wk.shape)],
        out_specs=[row(KV_LORA), row(QK_ROPE), row(KV_LORA), row(QK_ROPE), head(KV_LORA), head(QK_ROPE)],
        out_shape=[jax.ShapeDtypeStruct((rows, KV_LORA), F32),
                   jax.ShapeDtypeStruct((rows, QK_ROPE), F32),
                   jax.ShapeDtypeStruct((rows, KV_LORA), BF16),
                   jax.ShapeDtypeStruct((rows, QK_ROPE), BF16),
                   jax.ShapeDtypeStruct((N_HEADS, rows, KV_LORA), BF16),
                   jax.ShapeDtypeStruct((N_HEADS, rows, QK_ROPE), BF16)],
        compiler_params=_params(1),
        name="latent_query",
    )(x, cos, sin, wkva, kvn, wqa, qn, wqb, wk)


def _lane_tile(v, width):
    return jnp.concatenate([v] * (width // LANES), axis=1)


def _flash_kernel(ql_ref, qp_ref, kl_ref, kp_ref, ml_ref, mp_ref, wv_ref, o_ref, m_sc, l_sc, acc_sc):
    tq = ql_ref.shape[1]
    i = pl.program_id(1)
    units = [(h, pl.ds(r, FLASH_UNIT_ROWS)) for h in range(N_HEADS) for r in range(0, tq, FLASH_UNIT_ROWS)]

    def scores(u, keys):
        h, rows = units[u]
        return [_dot_nt(ql_ref[h, rows, :], kl) + _dot_nt(qp_ref[h, rows, :], kp) for kl, kp in keys]

    def max_part(u, s_list, first):
        h, rows = units[u]
        m_cur = jnp.max(s_list[0], axis=1, keepdims=True)
        for s in s_list[1:]:
            m_cur = jnp.maximum(m_cur, jnp.max(s, axis=1, keepdims=True))
        if first:
            return jnp.broadcast_to(m_cur, (FLASH_UNIT_ROWS, LANES)), None
        m_prev = m_sc[h, rows, :]
        m_new = jnp.maximum(m_prev, m_cur)
        return m_new, jnp.exp2(m_prev - m_new)

    def exp_part(u, s_list, m_new, alpha, first):
        h, rows = units[u]
        p_list = [jnp.exp2(s - _lane_tile(m_new, s.shape[1])) for s in s_list]
        p_sum = None
        for p in p_list:
            for c in range(p.shape[1] // LANES):
                blk = p[:, c * LANES:(c + 1) * LANES]
                p_sum = blk if p_sum is None else p_sum + blk
        l_sc[h, rows, :] = p_sum if first else alpha * l_sc[h, rows, :] + p_sum
        m_sc[h, rows, :] = m_new
        return [p.astype(BF16) for p in p_list]

    def value_part(u, p_list, alpha, keys, first):
        h, rows = units[u]
        pv = None
        for p, (kl, _) in zip(p_list, keys):
            d = _dot(p, kl)
            pv = d if pv is None else pv + d
        acc_sc[h, rows, :] = pv if first else _lane_tile(alpha, KV_LORA) * acc_sc[h, rows, :] + pv

    def step(unit_keys, first):
        s_q, m_q, p_q = {}, {}, {}
        for t in range(len(units) + FLASH_VALUE_LAG):
            if t < len(units):
                s_q[t] = scores(t, unit_keys(t)[0])
            u = t - FLASH_MAX_LAG
            if 0 <= u < len(units):
                s_q[u] = [s if mk is None else jnp.where(mk(s.shape), s, MASK_VALUE)
                          for s, mk in zip(s_q[u], unit_keys(u)[1])]
                m_q[u] = max_part(u, s_q[u], first)
            u = t - FLASH_EXP_LAG
            if 0 <= u < len(units):
                p_q[u] = exp_part(u, s_q.pop(u), *m_q[u], first)
            u = t - FLASH_VALUE_LAG
            if 0 <= u < len(units):
                value_part(u, p_q.pop(u), m_q.pop(u)[1], unit_keys(u)[0], first)

    def key_tile(start, size):
        off = pl.multiple_of(start, FLASH_UNIT_ROWS)
        return kl_ref[pl.ds(off, size), :], kp_ref[pl.ds(off, size), :]

    meta_mask = lambda shape: lax.broadcasted_iota(jnp.int32, shape, 1) < N_META
    causal_mask = lambda shape: (lax.broadcasted_iota(jnp.int32, shape, 1)
                                 <= lax.broadcasted_iota(jnp.int32, shape, 0))
    meta_keys = (ml_ref[...], mp_ref[...])
    diag_tiles = {r: key_tile(i * tq + r, FLASH_UNIT_ROWS) for r in range(0, tq, FLASH_UNIT_ROWS)}

    def first_keys(u):
        row0 = units[u][1].start
        below = list(range(0, row0, FLASH_UNIT_ROWS))
        return ([meta_keys] + [diag_tiles[r] for r in below] + [diag_tiles[row0]],
                [meta_mask] + [None] * len(below) + [causal_mask])

    step(first_keys, True)

    def body(j, carry):
        full_tile = key_tile(j * FLASH_KEY_ROWS, FLASH_KEY_ROWS)
        step(lambda u: ([full_tile], [None]), False)
        return carry

    lax.fori_loop(0, i * (tq // FLASH_KEY_ROWS), body, 0)

    for h in range(N_HEADS):
        l_row = jnp.sum(l_sc[h], axis=1, keepdims=True)
        o = (acc_sc[h] / l_row).astype(BF16)
        o_ref[:, h * V_HEAD:(h + 1) * V_HEAD] = _dot(o, wv_ref[h]).astype(BF16)


def _flash(qlat, qpe, klat, kpe, mlat, mpe, wv, bsz):
    rows = klat.shape[0]
    seq = rows // bsz
    tq = ATTN_TILE
    assert seq % tq == 0 and tq % FLASH_UNIT_ROWS == 0
    nq = seq // tq
    return pl.pallas_call(
        _flash_kernel,
        grid=(bsz, nq),
        in_specs=[pl.BlockSpec((N_HEADS, tq, KV_LORA), lambda b, i: (0, b * nq + i, 0)),
                  pl.BlockSpec((N_HEADS, tq, QK_ROPE), lambda b, i: (0, b * nq + i, 0)),
                  pl.BlockSpec((seq, KV_LORA), lambda b, i: (b, 0)),
                  pl.BlockSpec((seq, QK_ROPE), lambda b, i: (b, 0)),
                  _const_spec(mlat.shape), _const_spec(mpe.shape), _const_spec(wv.shape)],
        out_specs=pl.BlockSpec((tq, N_HEADS * V_HEAD), lambda b, i: (b * nq + i, 0)),
        out_shape=jax.ShapeDtypeStruct((rows, N_HEADS * V_HEAD), BF16),
        scratch_shapes=[pltpu.VMEM((N_HEADS, tq, LANES), F32), pltpu.VMEM((N_HEADS, tq, LANES), F32),
                        pltpu.VMEM((N_HEADS, tq, KV_LORA), F32)],
        compiler_params=_params(2),
        name="flash_prompt",
    )(qlat, qpe, klat, kpe, mlat, mpe, wv)


def _decode_kernel(pt_ref, ql_ref, qp_ref, kn_ref, pn_ref, ckv_hbm, kpe_hbm, o_ref, kbuf, pbuf, sem,
                   *, n_chunks):
    b = pl.program_id(0)
    n_samples = pl.num_programs(0)
    ahead = DECODE_SLOTS - 1

    def chunk_copies(sample, c):
        slot = (sample * n_chunks + c) % DECODE_SLOTS
        copies = []
        for p in range(PAGES_PER_CHUNK):
            page = pt_ref[sample, c * PAGES_PER_CHUNK + p]
            dst = pl.ds(p * PAGE_SIZE, PAGE_SIZE)
            copies.append(pltpu.make_async_copy(ckv_hbm.at[page], kbuf.at[slot, dst, :], sem.at[0, slot]))
            copies.append(pltpu.make_async_copy(kpe_hbm.at[page], pbuf.at[slot, :, dst], sem.at[1, slot]))
        return slot, copies

    @pl.when(b == 0)
    def _():
        for c in range(ahead):
            for cp in chunk_copies(0, c)[1]:
                cp.start()

    ql = ql_ref[0]
    qp = qp_ref[0]
    kn = kn_ref[0].astype(F32)
    pn = pn_ref[0].astype(F32)
    m_run = (jnp.sum(ql.astype(F32) * kn, axis=1, keepdims=True)
             + jnp.sum(qp.astype(F32) * pn, axis=1, keepdims=True))
    l_run = jnp.ones_like(m_run)
    acc = jnp.broadcast_to(kn, (N_HEADS, KV_LORA))

    for c in range(n_chunks):
        nxt_sample, nxt_c = b + (c + ahead) // n_chunks, (c + ahead) % n_chunks
        if c + ahead < n_chunks:
            for cp in chunk_copies(nxt_sample, nxt_c)[1]:
                cp.start()
        else:
            @pl.when(nxt_sample < n_samples)
            def _():
                for cp in chunk_copies(nxt_sample, nxt_c)[1]:
                    cp.start()
        slot, copies = chunk_copies(b, c)
        for cp in copies:
            cp.wait()
        k = kbuf[slot].astype(BF16)
        pe = pbuf[slot].astype(BF16)
        s = _dot_nt(ql, k) + _dot(qp, pe)
        m_new = jnp.maximum(m_run, jnp.max(s, axis=1, keepdims=True))
        alpha = jnp.exp2(m_run - m_new)
        p = jnp.exp2(s - m_new)
        l_run = alpha * l_run + jnp.sum(p, axis=1, keepdims=True)
        acc = alpha * acc + _dot(p.astype(BF16), k)
        m_run = m_new
    o_ref[0] = acc / l_run


def _decode_attention(page_table, qlat, qpe, knew, pnew, cache_ckv, cache_kpe):
    n_samples, n_pages = page_table.shape
    chunk_rows = PAGES_PER_CHUNK * PAGE_SIZE
    n_chunks = n_pages // PAGES_PER_CHUNK
    assert n_pages % PAGES_PER_CHUNK == 0 and n_chunks >= DECODE_SLOTS - 1
    per_sample = lambda width, lead: pl.BlockSpec((1, lead, width), lambda b, pt: (b, 0, 0))
    grid_spec = pltpu.PrefetchScalarGridSpec(
        num_scalar_prefetch=1,
        grid=(n_samples,),
        in_specs=[per_sample(KV_LORA, N_HEADS), per_sample(QK_ROPE, N_HEADS),
                  per_sample(KV_LORA, 1), per_sample(QK_ROPE, 1),
                  pl.BlockSpec(memory_space=pl.ANY), pl.BlockSpec(memory_space=pl.ANY)],
        out_specs=per_sample(KV_LORA, N_HEADS),
        scratch_shapes=[pltpu.VMEM((DECODE_SLOTS, chunk_rows, KV_LORA), F32),
                        pltpu.VMEM((DECODE_SLOTS, QK_ROPE, chunk_rows), F32),
                        pltpu.SemaphoreType.DMA((2, DECODE_SLOTS))],
    )
    return pl.pallas_call(
        functools.partial(_decode_kernel, n_chunks=n_chunks),
        grid_spec=grid_spec,
        out_shape=jax.ShapeDtypeStruct((n_samples, N_HEADS, KV_LORA), F32),
        compiler_params=_params(1),
        name="decode_attention",
    )(page_table, qlat, qpe, knew, pnew, cache_ckv, cache_kpe)


def _value_proj_kernel(o_ref, wv_ref, a_ref):
    for h in range(N_HEADS):
        a_ref[:, h * V_HEAD:(h + 1) * V_HEAD] = _dot(o_ref[h].astype(BF16), wv_ref[h]).astype(BF16)


def _value_proj(o, wv):
    rows = o.shape[1]
    return pl.pallas_call(
        _value_proj_kernel,
        grid=(1,),
        in_specs=[_const_spec(o.shape), _const_spec(wv.shape)],
        out_specs=_const_spec((rows, N_HEADS * V_HEAD)),
        out_shape=jax.ShapeDtypeStruct((rows, N_HEADS * V_HEAD), BF16),
        compiler_params=_params(1),
        name="value_proj",
    )(o, wv)


def _rope_tables(pos):
    half = QK_ROPE // 2
    inv = ROPE_THETA ** (-jnp.arange(half, dtype=F32) / half)
    ang = pos.astype(F32)[:, None] * inv[None, :]
    cos, sin = jnp.cos(ang), jnp.sin(ang)
    return jnp.concatenate([cos, cos], axis=1), jnp.concatenate([-sin, sin], axis=1)


def _rotate_half_cols(w):
    half = QK_ROPE // 2
    return jnp.concatenate([w[..., half:], w[..., :half]], axis=-1)


def kernel(x_prompt, x_sample, state_pool, cache_ckv, cache_kpe, page_table, meta_tokens,
           ln_g, ln_b, pool_w, pool_scale, w_gate, w_up, w_down,
           wq_a, q_norm, wq_b, wo, wkv_a, kv_norm, wkv_b):
    bsz, seq, _ = x_prompt.shape
    n_samples = x_sample.shape[0]
    past_len = page_table.shape[1] * PAGE_SIZE

    vec = lambda v: v.reshape(1, -1).astype(F32)
    pool_wb = pool_w[0].astype(BF16)
    wg, wu, wd = w_gate.astype(BF16), w_up.astype(BF16), w_down.astype(BF16)
    ffn_w = lambda l: (wg, wu, wd, l)
    wkva = jnp.concatenate([wkv_a, _rotate_half_cols(wkv_a[:, KV_LORA:])], axis=1).astype(BF16)
    wqb3 = wq_b[0].reshape(wq_b.shape[1], N_HEADS, QK_NOPE + QK_ROPE)
    wqb = jnp.concatenate([wqb3, _rotate_half_cols(wqb3[..., QK_NOPE:])], axis=-1)
    wqb = wqb.reshape(wq_b.shape[1], -1).astype(BF16)
    wk = jnp.transpose(wkv_b[:, :, :QK_NOPE], (1, 2, 0)).astype(BF16)
    wv = jnp.transpose(wkv_b[:, :, QK_NOPE:], (1, 0, 2)).astype(BF16)
    wqa, wob = wq_a[0].astype(BF16), wo[0].astype(BF16)
    ln = lambda l, k: (vec(ln_g[l, k]), vec(ln_b[l, k]))

    meta = meta_tokens.astype(F32)
    pool_args = (pool_wb, vec(pool_scale[0])) + ln(0, 0)
    ffn0 = ffn_w(0) + ln(0, 1)
    x2_main = _ffn(x_prompt.reshape(bsz * seq, D_MODEL), *ffn0, pool=(meta,) + pool_args + (seq,))
    x1_meta = _pool_meta(meta, *pool_args)
    xs_sample = jnp.concatenate([jnp.swapaxes(state_pool[0], 0, 1), jnp.swapaxes(x_sample, 0, 1)], axis=0)
    x1_sample = _pool_sample(xs_sample, *pool_args)
    x2_small = _ffn(jnp.concatenate([x1_sample, x1_meta], axis=0), *ffn0)

    cos_main, sin_main = _rope_tables(N_META + jnp.arange(seq))
    pos_small = jnp.concatenate([jnp.full((n_samples,), past_len), jnp.arange(N_META)])
    cos_small, sin_small = _rope_tables(pos_small)
    lq_w = (wkva, vec(kv_norm), wqa, vec(q_norm[0]), wqb, wk)
    ckv_main, kpe_main, klat_main, kpeb_main, qlat_main, qpe_main = _latent_query(
        x2_main, cos_main, sin_main, *lq_w)
    ckv_small, kpe_small, klat_small, kpeb_small, qlat_small, qpe_small = _latent_query(
        x2_small, cos_small, sin_small, *lq_w)

    pad_keys = lambda k: jnp.pad(k[n_samples:], ((0, LANES - N_META), (0, 0)))
    a_main = _flash(qlat_main, qpe_main, klat_main, kpeb_main,
                    pad_keys(klat_small), pad_keys(kpeb_small), wv, bsz)
    o_sample = _decode_attention(
        page_table,
        jnp.swapaxes(qlat_small[:, :n_samples], 0, 1), jnp.swapaxes(qpe_small[:, :n_samples], 0, 1),
        klat_small[:n_samples, None, :], kpeb_small[:n_samples, None, :],
        cache_ckv, jnp.swapaxes(cache_kpe, 1, 2))
    a_sample = _value_proj(jnp.swapaxes(o_sample, 0, 1), wv)

    ffn1 = ffn_w(1) + ln(1, 1)
    y_main = _ffn(x2_main, *ffn1, proj=(a_main, wob) + ln(1, 0))
    y_sample = _ffn(x2_small[:n_samples], *ffn1, proj=(a_sample, wob) + ln(1, 0))

    pool_prompt = x_prompt[None, :, seq - POOL_HIST:, :]
    pool_sample = jnp.concatenate([state_pool[0][:, 1:], x_sample], axis=1)[None]
    bcast = lambda t: jnp.broadcast_to(t[n_samples:][None], (bsz, N_META, t.shape[1]))
    ckv_prompt = jnp.concatenate([bcast(ckv_small), ckv_main.reshape(bsz, seq, KV_LORA)], axis=1)
    kpe_prompt = jnp.concatenate([bcast(kpe_small), kpe_main.reshape(bsz, seq, QK_ROPE)], axis=1)
    return (y_main.reshape(bsz, seq, D_MODEL), y_sample.reshape(n_samples, 1, D_MODEL),
            pool_prompt, pool_sample, ckv_prompt, kpe_prompt,
            ckv_small[:n_samples, None, :], kpe_small[:n_samples, None, :])
```

```python
import functools

import jax
import jax.numpy as jnp
from jax import lax
from jax.experimental import pallas as pl
from jax.experimental.pallas import tpu as pltpu

F32 = jnp.float32
BF16 = jnp.bfloat16

D_MODEL = 1024
DEPTH = 2
N_META = 16
POOL_WINDOWS = (2, 4, 8, 16)
POOL_GROUP = D_MODEL // len(POOL_WINDOWS)
POOL_HIST = max(POOL_WINDOWS) - 1
N_HEADS = 8
QK_NOPE = 128
QK_ROPE = 64
V_HEAD = 128
KV_LORA = 256
ROPE_THETA = 10000.0
ALPHA = (2 * DEPTH) ** 0.25
LN_EPS = 1e-5
RMS_EPS = 1e-6
SM_SCALE = (QK_NOPE + QK_ROPE) ** -0.5
PAGE_SIZE = 128

LANES = 128
MASK_VALUE = -1e30
VMEM_LIMIT = 48 * 1024 * 1024

ROW_TILE = 512
FFN_ROW_TILE = 1024
FFN_UNIT_ROWS = 512
FFN_VMEM_LIMIT = 58 * 1024 * 1024
ATTN_TILE = 1024
FLASH_UNIT_ROWS = 256
FLASH_KEY_ROWS = 512
FLASH_MAX_LAG = 1
FLASH_EXP_LAG = 2
FLASH_VALUE_LAG = 3
LOG2_E = 1.4426950408889634
PAGES_PER_CHUNK = 32
DECODE_SLOTS = 3


def _dot(a, b):
    return jnp.dot(a, b, preferred_element_type=F32)


def _dot_nt(a, b):
    return lax.dot_general(a, b, (((1,), (1,)), ((), ())), preferred_element_type=F32)


def _layer_norm(x, g, b):
    mu = jnp.mean(x, axis=-1, keepdims=True)
    xc = x - mu
    var = jnp.mean(xc * xc, axis=-1, keepdims=True)
    return xc * lax.rsqrt(var + LN_EPS) * g + b


def _rms_norm(x, g):
    return x * lax.rsqrt(jnp.mean(x * x, axis=-1, keepdims=True) + RMS_EPS) * g


def _stagger(stage_generators):
    live = dict(enumerate(stage_generators))
    t = 0
    while live:
        for k in sorted(live):
            if t >= k:
                try:
                    next(live[k])
                except StopIteration:
                    del live[k]
        t += 1


def _const_spec(shape):
    zeros = (0,) * len(shape)
    return pl.BlockSpec(shape, lambda *_: zeros, pipeline_mode=pl.Buffered(1))


def _params(n_axes):
    return pltpu.CompilerParams(dimension_semantics=("arbitrary",) * n_axes,
                                vmem_limit_bytes=VMEM_LIMIT)


def _window_sums(xs, n_rows, pad):
    hi = xs.astype(BF16)
    lo = (xs - hi.astype(F32)).astype(BF16)
    k = xs.shape[0]
    diff = (lax.broadcasted_iota(jnp.int32, (n_rows, k), 0) + pad
            - lax.broadcasted_iota(jnp.int32, (n_rows, k), 1))
    sums = []
    for g, win in enumerate(POOL_WINDOWS):
        band = jnp.where(diff >= 0, jnp.where(diff < win, 1.0, 0.0), 0.0).astype(BF16)
        cols = slice(g * POOL_GROUP, (g + 1) * POOL_GROUP)
        sums.append(_dot(band, hi[:, cols]) + _dot(band, lo[:, cols]))
    return sums


def _pool_finish(x, means, w_ref, scale, g, b):
    outs = []
    for gi in range(len(POOL_WINDOWS)):
        cols = slice(gi * POOL_GROUP, (gi + 1) * POOL_GROUP)
        d = means[gi] - x[:, cols]
        outs.append(_dot(d.astype(BF16), w_ref[gi]))
    h = jnp.concatenate(outs, axis=1) * scale
    return _layer_norm(ALPHA * x + h, g, b)


def _blocked_window_means(xs, n_rows):
    hi = xs.astype(BF16)
    lo = (xs - hi.astype(F32)).astype(BF16)
    diff = (lax.broadcasted_iota(jnp.int32, (LANES, 2 * LANES), 0) + LANES
            - lax.broadcasted_iota(jnp.int32, (LANES, 2 * LANES), 1))
    means = []
    for g, win in enumerate(POOL_WINDOWS):
        band = jnp.where(diff >= 0, jnp.where(diff < win, 1.0, 0.0), 0.0).astype(BF16)
        cols = slice(g * POOL_GROUP, (g + 1) * POOL_GROUP)
        blocks = []
        for r in range(0, n_rows, LANES):
            rows = slice(r, r + 2 * LANES)
            blocks.append(_dot(band, hi[rows, cols]) + _dot(band, lo[rows, cols]))
        means.append(jnp.concatenate(blocks, axis=0) * (1.0 / win))
    return means


def _pool_main_rows(x, halo, w_ref, scale, g, b):
    xs = jnp.concatenate([jnp.zeros((LANES - N_META, D_MODEL), F32), halo, x], axis=0)
    return _pool_finish(x, _blocked_window_means(xs, x.shape[0]), w_ref, scale, g, b)


def _pool_meta_kernel(x_ref, w_ref, scale_ref, g_ref, b_ref, o_ref):
    x = x_ref[...]
    n = x.shape[0]
    xs = jnp.concatenate([jnp.zeros((LANES - n, D_MODEL), F32), x], axis=0)
    sums = _window_sums(xs, n, LANES - n)
    row = lax.broadcasted_iota(jnp.int32, (n, 1), 0)
    means = [s / jnp.minimum(row + 1, w).astype(F32) for s, w in zip(sums, POOL_WINDOWS)]
    o_ref[...] = _pool_finish(x, means, w_ref, scale_ref[...], g_ref[...], b_ref[...])


def _pool_sample_kernel(xs_ref, w_ref, scale_ref, g_ref, b_ref, o_ref):
    n_ctx = xs_ref.shape[0]
    x = xs_ref[n_ctx - 1]
    means = []
    for gi, win in enumerate(POOL_WINDOWS):
        cols = slice(gi * POOL_GROUP, (gi + 1) * POOL_GROUP)
        s = xs_ref[n_ctx - win, :, cols]
        for k in range(n_ctx - win + 1, n_ctx):
            s = s + xs_ref[k, :, cols]
        means.append(s * (1.0 / win))
    o_ref[...] = _pool_finish(x, means, w_ref, scale_ref[...], g_ref[...], b_ref[...])


def _pool_weight_specs():
    return [_const_spec((len(POOL_WINDOWS), POOL_GROUP, POOL_GROUP)),
            _const_spec((1, D_MODEL)), _const_spec((1, D_MODEL)), _const_spec((1, D_MODEL))]


def _pool_meta(meta, pool_w, scale, g, b):
    return pl.pallas_call(
        _pool_meta_kernel,
        grid=(1,),
        in_specs=[_const_spec(meta.shape)] + _pool_weight_specs(),
        out_specs=_const_spec(meta.shape),
        out_shape=jax.ShapeDtypeStruct(meta.shape, F32),
        compiler_params=_params(1),
        name="pool_meta",
    )(meta, pool_w, scale, g, b)


def _pool_sample(xs, pool_w, scale, g, b):
    rows = xs.shape[1]
    return pl.pallas_call(
        _pool_sample_kernel,
        grid=(1,),
        in_specs=[_const_spec(xs.shape)] + _pool_weight_specs(),
        out_specs=_const_spec((rows, D_MODEL)),
        out_shape=jax.ShapeDtypeStruct((rows, D_MODEL), F32),
        compiler_params=_params(1),
        name="pool_sample",
    )(xs, pool_w, scale, g, b)


def _ffn_kernel(*refs, mixer, tiles_per_seq, unit_rows):
    if mixer == "proj":
        a_ref, x_ref, wo_ref, g1_ref, b1_ref = refs[:5]
        refs = refs[5:]
    elif mixer == "pool":
        x_ref, prev_ref, meta_ref, pw_ref, scale_ref, g1_ref, b1_ref = refs[:7]
        refs = refs[7:]
    else:
        x_ref = refs[0]
        refs = refs[1:]
    wg_ref, wu_ref, wd_ref, g2_ref, b2_ref, o_ref = refs

    def unit_stages(r0):
        rows = slice(r0, r0 + unit_rows)
        if mixer == "proj":
            x = _layer_norm(ALPHA * x_ref[rows, :] + _dot(a_ref[rows, :], wo_ref[...]), g1_ref[...], b1_ref[...])
        elif mixer == "pool":
            if r0 == 0:
                halo = jnp.where(pl.program_id(0) % tiles_per_seq == 0, meta_ref[...], prev_ref[...])
            else:
                halo = x_ref[r0 - N_META:r0, :]
            x = _pool_main_rows(x_ref[rows, :], halo, pw_ref, scale_ref[...], g1_ref[...], b1_ref[...])
        else:
            x = x_ref[rows, :]
        xb = x.astype(BF16)
        yield
        gate = _dot(xb, wg_ref[...])
        up = _dot(xb, wu_ref[...])
        mid = (gate * (1.0 / (1.0 + jnp.exp(-gate))) * up).astype(BF16)
        ffn = _dot(mid, wd_ref[...])
        yield
        o_ref[rows, :] = _layer_norm(ALPHA * x + ffn, g2_ref[...], b2_ref[...])
        yield

    _stagger([unit_stages(r0) for r0 in range(0, x_ref.shape[0], unit_rows)])


def _ffn(x, wg, wu, wd, layer, g2, b2, proj=None, pool=None):
    rows = x.shape[0]
    layer_spec = lambda w: pl.BlockSpec((None,) + w.shape[1:], lambda *_: (layer, 0, 0),
                                        pipeline_mode=pl.Buffered(1))
    tile = min(FFN_ROW_TILE, rows)
    row_spec = pl.BlockSpec((tile, D_MODEL), lambda i: (i, 0))
    vec_spec = _const_spec((1, D_MODEL))
    args, specs = [], []
    mixer, tiles_per_seq = None, None
    if proj is not None:
        mixer = "proj"
        a, wo, g1, b1 = proj
        args += [a, x, wo, g1, b1]
        specs += [row_spec, row_spec, _const_spec(wo.shape), vec_spec, vec_spec]
    elif pool is not None:
        mixer = "pool"
        meta, pool_w, scale, g1, b1, seq = pool
        assert seq % tile == 0 and tile % LANES == 0
        tiles_per_seq = seq // tile
        halo_blocks = tile // N_META
        args += [x, x, meta, pool_w, scale, g1, b1]
        specs += [row_spec,
                  pl.BlockSpec((N_META, D_MODEL), lambda i: (jnp.maximum(i * halo_blocks - 1, 0), 0)),
                  _const_spec(meta.shape)] + _pool_weight_specs()
    else:
        args += [x]
        specs += [row_spec]
    args += [wg, wu, wd, g2, b2]
    specs += [layer_spec(wg), layer_spec(wu), layer_spec(wd), vec_spec, vec_spec]
    return pl.pallas_call(
        functools.partial(_ffn_kernel, mixer=mixer, tiles_per_seq=tiles_per_seq,
                          unit_rows=FFN_UNIT_ROWS if tile % FFN_UNIT_ROWS == 0 else tile),
        grid=(rows // tile,),
        in_specs=specs,
        out_specs=row_spec,
        out_shape=jax.ShapeDtypeStruct((rows, D_MODEL), F32),
        compiler_params=pltpu.CompilerParams(dimension_semantics=("arbitrary",),
                                             vmem_limit_bytes=FFN_VMEM_LIMIT),
        name="ffn" if mixer is None else "ffn_" + mixer,
    )(*args)


def _latent_query_kernel(x_ref, cos_ref, sin_ref, wkva_ref, kvn_ref, wqa_ref, qn_ref, wqb_ref, wk_ref,
                         ckv_ref, kpe_ref, klat_ref, kpeb_ref, qlat_ref, qpe_ref):
    xb = x_ref[...].astype(BF16)
    cos = cos_ref[...]
    sin = sin_ref[...]
    kv = _dot(xb, wkva_ref[...])
    ckv = _rms_norm(kv[:, :KV_LORA], kvn_ref[...])
    kpe = kv[:, KV_LORA:KV_LORA + QK_ROPE] * cos + kv[:, KV_LORA + QK_ROPE:] * sin
    ckv_ref[...] = ckv
    kpe_ref[...] = kpe
    klat_ref[...] = ckv.astype(BF16)
    kpeb_ref[...] = kpe.astype(BF16)
    qa = _rms_norm(_dot(xb, wqa_ref[...]), qn_ref[...]).astype(BF16)
    q = _dot(qa, wqb_ref[...])
    per_head = QK_NOPE + 2 * QK_ROPE
    for h in range(N_HEADS):
        qh = q[:, h * per_head:(h + 1) * per_head]
        qlat = _dot(qh[:, :QK_NOPE].astype(BF16), wk_ref[h]) * (SM_SCALE * LOG2_E)
        qpe = (qh[:, QK_NOPE:QK_NOPE + QK_ROPE] * cos + qh[:, QK_NOPE + QK_ROPE:] * sin) * (SM_SCALE * LOG2_E)
        qlat_ref[h] = qlat.astype(BF16)
        qpe_ref[h] = qpe.astype(BF16)


def _latent_query(x, cos, sin, wkva, kvn, wqa, qn, wqb, wk):
    rows = x.shape[0]
    tile = min(ROW_TILE, rows)
    pos_tiles = cos.shape[0] // tile
    row = lambda width: pl.BlockSpec((tile, width), lambda i: (i, 0))
    pos = pl.BlockSpec((tile, QK_ROPE), lambda i: (i % pos_tiles, 0))
    head = lambda width: pl.BlockSpec((N_HEADS, tile, width), lambda i: (0, i, 0))
    return pl.pallas_call(
        _latent_query_kernel,
        grid=(rows // tile,),
        in_specs=[row(D_MODEL), pos, pos, _const_spec(wkva.shape), _const_spec(kvn.shape),
                  _const_spec(wqa.shape), _const_spec(qn.shape), _const_spec(wqb.shape),
                  _const_spec(wk.shape)],
        out_specs=[row(KV_LORA), row(QK_ROPE), row(KV_LORA), row(QK_ROPE), head(KV_LORA), head(QK_ROPE)],
        out_shape=[jax.ShapeDtypeStruct((rows, KV_LORA), F32),
                   jax.ShapeDtypeStruct((rows, QK_ROPE), F32),
                   jax.ShapeDtypeStruct((rows, KV_LORA), BF16),
                   jax.ShapeDtypeStruct((rows, QK_ROPE), BF16),
                   jax.ShapeDtypeStruct((N_HEADS, rows, KV_LORA), BF16),
                   jax.ShapeDtypeStruct((N_HEADS, rows, QK_ROPE), BF16)],
        compiler_params=_params(1),
        name="latent_query",
    )(x, cos, sin, wkva, kvn, wqa, qn, wqb, wk)


def _lane_tile(v, width):
    return jnp.concatenate([v] * (width // LANES), axis=1)


def _flash_kernel(ql_ref, qp_ref, kl_ref, kp_ref, ml_ref, mp_ref, wv_ref, o_ref, m_sc, l_sc, acc_sc):
    tq = ql_ref.shape[1]
    i = pl.program_id(1)
    units = [(h, pl.ds(r, FLASH_UNIT_ROWS)) for h in range(N_HEADS) for r in range(0, tq, FLASH_UNIT_ROWS)]

    def scores(u, keys):
        h, rows = units[u]
        return [_dot_nt(ql_ref[h, rows, :], kl) + _dot_nt(qp_ref[h, rows, :], kp) for kl, kp in keys]

    def max_part(u, s_list, first):
        h, rows = units[u]
        m_cur = jnp.max(s_list[0], axis=1, keepdims=True)
        for s in s_list[1:]:
            m_cur = jnp.maximum(m_cur, jnp.max(s, axis=1, keepdims=True))
        if first:
            return jnp.broadcast_to(m_cur, (FLASH_UNIT_ROWS, LANES)), None
        m_prev = m_sc[h, rows, :]
        m_new = jnp.maximum(m_prev, m_cur)
        return m_new, jnp.exp2(m_prev - m_new)

    def exp_part(u, s_list, m_new, alpha, first):
        h, rows = units[u]
        p_list = [jnp.exp2(s - _lane_tile(m_new, s.shape[1])) for s in s_list]
        p_sum = None
        for p in p_list:
            for c in range(p.shape[1] // LANES):
                blk = p[:, c * LANES:(c + 1) * LANES]
                p_sum = blk if p_sum is None else p_sum + blk
        l_sc[h, rows, :] = p_sum if first else alpha * l_sc[h, rows, :] + p_sum
        m_sc[h, rows, :] = m_new
        return [p.astype(BF16) for p in p_list]

    def value_part(u, p_list, alpha, keys, first):
        h, rows = units[u]
        pv = None
        for p, (kl, _) in zip(p_list, keys):
            d = _dot(p, kl)
            pv = d if pv is None else pv + d
        acc_sc[h, rows, :] = pv if first else _lane_tile(alpha, KV_LORA) * acc_sc[h, rows, :] + pv

    def step(unit_keys, first):
        s_q, m_q, p_q = {}, {}, {}
        for t in range(len(units) + FLASH_VALUE_LAG):
            if t < len(units):
                s_q[t] = scores(t, unit_keys(t)[0])
            u = t - FLASH_MAX_LAG
            if 0 <= u < len(units):
                s_q[u] = [s if mk is None else jnp.where(mk(s.shape), s, MASK_VALUE)
                          for s, mk in zip(s_q[u], unit_keys(u)[1])]
                m_q[u] = max_part(u, s_q[u], first)
            u = t - FLASH_EXP_LAG
            if 0 <= u < len(units):
                p_q[u] = exp_part(u, s_q.pop(u), *m_q[u], first)
            u = t - FLASH_VALUE_LAG
            if 0 <= u < len(units):
                value_part(u, p_q.pop(u), m_q.pop(u)[1], unit_keys(u)[0], first)

    def key_tile(start, size):
        off = pl.multiple_of(start, FLASH_UNIT_ROWS)
        return kl_ref[pl.ds(off, size), :], kp_ref[pl.ds(off, size), :]

    meta_mask = lambda shape: lax.broadcasted_iota(jnp.int32, shape, 1) < N_META
    causal_mask = lambda shape: (lax.broadcasted_iota(jnp.int32, shape, 1)
                                 <= lax.broadcasted_iota(jnp.int32, shape, 0))
    meta_keys = (ml_ref[...], mp_ref[...])
    diag_tiles = {r: key_tile(i * tq + r, FLASH_UNIT_ROWS) for r in range(0, tq, FLASH_UNIT_ROWS)}

    def first_keys(u):
        row0 = units[u][1].start
        below = list(range(0, row0, FLASH_UNIT_ROWS))
        return ([meta_keys] + [diag_tiles[r] for r in below] + [diag_tiles[row0]],
                [meta_mask] + [None] * len(below) + [causal_mask])

    step(first_keys, True)

    def body(j, carry):
        full_tile = key_tile(j * FLASH_KEY_ROWS, FLASH_KEY_ROWS)
        step(lambda u: ([full_tile], [None]), False)
        return carry

    lax.fori_loop(0, i * (tq // FLASH_KEY_ROWS), body, 0)

    for h in range(N_HEADS):
        l_row = jnp.sum(l_sc[h], axis=1, keepdims=True)
        o = (acc_sc[h] / l_row).astype(BF16)
        o_ref[:, h * V_HEAD:(h + 1) * V_HEAD] = _dot(o, wv_ref[h]).astype(BF16)


def _flash(qlat, qpe, klat, kpe, mlat, mpe, wv, bsz):
    rows = klat.shape[0]
    seq = rows // bsz
    tq = ATTN_TILE
    assert seq % tq == 0 and tq % FLASH_UNIT_ROWS == 0
    nq = seq // tq
    return pl.pallas_call(
        _flash_kernel,
        grid=(bsz, nq),
        in_specs=[pl.BlockSpec((N_HEADS, tq, KV_LORA), lambda b, i: (0, b * nq + i, 0)),
                  pl.BlockSpec((N_HEADS, tq, QK_ROPE), lambda b, i: (0, b * nq + i, 0)),
                  pl.BlockSpec((seq, KV_LORA), lambda b, i: (b, 0)),
                  pl.BlockSpec((seq, QK_ROPE), lambda b, i: (b, 0)),
                  _const_spec(mlat.shape), _const_spec(mpe.shape), _const_spec(wv.shape)],
        out_specs=pl.BlockSpec((tq, N_HEADS * V_HEAD), lambda b, i: (b * nq + i, 0)),
        out_shape=jax.ShapeDtypeStruct((rows, N_HEADS * V_HEAD), BF16),
        scratch_shapes=[pltpu.VMEM((N_HEADS, tq, LANES), F32), pltpu.VMEM((N_HEADS, tq, LANES), F32),
                        pltpu.VMEM((N_HEADS, tq, KV_LORA), F32)],
        compiler_params=_params(2),
        name="flash_prompt",
    )(qlat, qpe, klat, kpe, mlat, mpe, wv)


def _decode_kernel(pt_ref, ql_ref, qp_ref, kn_ref, pn_ref, ckv_hbm, kpe_hbm, o_ref, kbuf, pbuf, sem,
                   *, n_chunks):
    b = pl.program_id(0)
    n_samples = pl.num_programs(0)
    ahead = DECODE_SLOTS - 1

    def chunk_copies(sample, c):
        slot = (sample * n_chunks + c) % DECODE_SLOTS
        copies = []
        for p in range(PAGES_PER_CHUNK):
            page = pt_ref[sample, c * PAGES_PER_CHUNK + p]
            dst = pl.ds(p * PAGE_SIZE, PAGE_SIZE)
            copies.append(pltpu.make_async_copy(ckv_hbm.at[page], kbuf.at[slot, dst, :], sem.at[0, slot]))
            copies.append(pltpu.make_async_copy(kpe_hbm.at[page], pbuf.at[slot, :, dst], sem.at[1, slot]))
        return slot, copies

    @pl.when(b == 0)
    def _():
        for c in range(ahead):
            for cp in chunk_copies(0, c)[1]:
                cp.start()

    ql = ql_ref[0]
    qp = qp_ref[0]
    kn = kn_ref[0].astype(F32)
    pn = pn_ref[0].astype(F32)
    m_run = (jnp.sum(ql.astype(F32) * kn, axis=1, keepdims=True)
             + jnp.sum(qp.astype(F32) * pn, axis=1, keepdims=True))
    l_run = jnp.ones_like(m_run)
    acc = jnp.broadcast_to(kn, (N_HEADS, KV_LORA))

    for c in range(n_chunks):
        nxt_sample, nxt_c = b + (c + ahead) // n_chunks, (c + ahead) % n_chunks
        if c + ahead < n_chunks:
            for cp in chunk_copies(nxt_sample, nxt_c)[1]:
                cp.start()
        else:
            @pl.when(nxt_sample < n_samples)
            def _():
                for cp in chunk_copies(nxt_sample, nxt_c)[1]:
                    cp.start()
        slot, copies = chunk_copies(b, c)
        for cp in copies:
            cp.wait()
        k = kbuf[slot].astype(BF16)
        pe = pbuf[slot].astype(BF16)
        s = _dot_nt(ql, k) + _dot(qp, pe)
        m_new = jnp.maximum(m_run, jnp.max(s, axis=1, keepdims=True))
        alpha = jnp.exp2(m_run - m_new)
        p = jnp.exp2(s - m_new)
        l_run = alpha * l_run + jnp.sum(p, axis=1, keepdims=True)
        acc = alpha * acc + _dot(p.astype(BF16), k)
        m_run = m_new
    o_ref[0] = acc / l_run


def _decode_attention(page_table, qlat, qpe, knew, pnew, cache_ckv, cache_kpe):
    n_samples, n_pages = page_table.shape
    chunk_rows = PAGES_PER_CHUNK * PAGE_SIZE
    n_chunks = n_pages // PAGES_PER_CHUNK
    assert n_pages % PAGES_PER_CHUNK == 0 and n_chunks >= DECODE_SLOTS - 1
    per_sample = lambda width, lead: pl.BlockSpec((1, lead, width), lambda b, pt: (b, 0, 0))
    grid_spec = pltpu.PrefetchScalarGridSpec(
        num_scalar_prefetch=1,
        grid=(n_samples,),
        in_specs=[per_sample(KV_LORA, N_HEADS), per_sample(QK_ROPE, N_HEADS),
                  per_sample(KV_LORA, 1), per_sample(QK_ROPE, 1),
                  pl.BlockSpec(memory_space=pl.ANY), pl.BlockSpec(memory_space=pl.ANY)],
        out_specs=per_sample(KV_LORA, N_HEADS),
        scratch_shapes=[pltpu.VMEM((DECODE_SLOTS, chunk_rows, KV_LORA), F32),
                        pltpu.VMEM((DECODE_SLOTS, QK_ROPE, chunk_rows), F32),
                        pltpu.SemaphoreType.DMA((2, DECODE_SLOTS))],
    )
    return pl.pallas_call(
        functools.partial(_decode_kernel, n_chunks=n_chunks),
        grid_spec=grid_spec,
        out_shape=jax.ShapeDtypeStruct((n_samples, N_HEADS, KV_LORA), F32),
        compiler_params=_params(1),
        name="decode_attention",
    )(page_table, qlat, qpe, knew, pnew, cache_ckv, cache_kpe)


def _value_proj_kernel(o_ref, wv_ref, a_ref):
    for h in range(N_HEADS):
        a_ref[:, h * V_HEAD:(h + 1) * V_HEAD] = _dot(o_ref[h].astype(BF16), wv_ref[h]).astype(BF16)


def _value_proj(o, wv):
    rows = o.shape[1]
    return pl.pallas_call(
        _value_proj_kernel,
        grid=(1,),
        in_specs=[_const_spec(o.shape), _const_spec(wv.shape)],
        out_specs=_const_spec((rows, N_HEADS * V_HEAD)),
        out_shape=jax.ShapeDtypeStruct((rows, N_HEADS * V_HEAD), BF16),
        compiler_params=_params(1),
        name="value_proj",
    )(o, wv)


def _rope_tables(pos):
    half = QK_ROPE // 2
    inv = ROPE_THETA ** (-jnp.arange(half, dtype=F32) / half)
    ang = pos.astype(F32)[:, None] * inv[None, :]
    cos, sin = jnp.cos(ang), jnp.sin(ang)
    return jnp.concatenate([cos, cos], axis=1), jnp.concatenate([-sin, sin], axis=1)


def _rotate_half_cols(w):
    half = QK_ROPE // 2
    return jnp.concatenate([w[..., half:], w[..., :half]], axis=-1)


def kernel(x_prompt, x_sample, state_pool, cache_ckv, cache_kpe, page_table, meta_tokens,
           ln_g, ln_b, pool_w, pool_scale, w_gate, w_up, w_down,
           wq_a, q_norm, wq_b, wo, wkv_a, kv_norm, wkv_b):
    bsz, seq, _ = x_prompt.shape
    n_samples = x_sample.shape[0]
    past_len = page_table.shape[1] * PAGE_SIZE

    vec = lambda v: v.reshape(1, -1).astype(F32)
    pool_wb = pool_w[0].astype(BF16)
    wg, wu, wd = w_gate.astype(BF16), w_up.astype(BF16), w_down.astype(BF16)
    ffn_w = lambda l: (wg, wu, wd, l)
    wkva = jnp.concatenate([wkv_a, _rotate_half_cols(wkv_a[:, KV_LORA:])], axis=1).astype(BF16)
    wqb3 = wq_b[0].reshape(wq_b.shape[1], N_HEADS, QK_NOPE + QK_ROPE)
    wqb = jnp.concatenate([wqb3, _rotate_half_cols(wqb3[..., QK_NOPE:])], axis=-1)
    wqb = wqb.reshape(wq_b.shape[1], -1).astype(BF16)
    wk = jnp.transpose(wkv_b[:, :, :QK_NOPE], (1, 2, 0)).astype(BF16)
    wv = jnp.transpose(wkv_b[:, :, QK_NOPE:], (1, 0, 2)).astype(BF16)
    wqa, wob = wq_a[0].astype(BF16), wo[0].astype(BF16)
    ln = lambda l, k: (vec(ln_g[l, k]), vec(ln_b[l, k]))

    meta = meta_tokens.astype(F32)
    pool_args = (pool_wb, vec(pool_scale[0])) + ln(0, 0)
    ffn0 = ffn_w(0) + ln(0, 1)
    x2_main = _ffn(x_prompt.reshape(bsz * seq, D_MODEL), *ffn0, pool=(meta,) + pool_args + (seq,))
    x1_meta = _pool_meta(meta, *pool_args)
    xs_sample = jnp.concatenate([jnp.swapaxes(state_pool[0], 0, 1), jnp.swapaxes(x_sample, 0, 1)], axis=0)
    x1_sample = _pool_sample(xs_sample, *pool_args)
    x2_small = _ffn(jnp.concatenate([x1_sample, x1_meta], axis=0), *ffn0)

    cos_main, sin_main = _rope_tables(N_META + jnp.arange(seq))
    pos_small = jnp.concatenate([jnp.full((n_samples,), past_len), jnp.arange(N_META)])
    cos_small, sin_small = _rope_tables(pos_small)
    lq_w = (wkva, vec(kv_norm), wqa, vec(q_norm[0]), wqb, wk)
    ckv_main, kpe_main, klat_main, kpeb_main, qlat_main, qpe_main = _latent_query(
        x2_main, cos_main, sin_main, *lq_w)
    ckv_small, kpe_small, klat_small, kpeb_small, qlat_small, qpe_small = _latent_query(
        x2_small, cos_small, sin_small, *lq_w)

    pad_keys = lambda k: jnp.pad(k[n_samples:], ((0, LANES - N_META), (0, 0)))
    a_main = _flash(qlat_main, qpe_main, klat_main, kpeb_main,
                    pad_keys(klat_small), pad_keys(kpeb_small), wv, bsz)
    o_sample = _decode_attention(
        page_table,
        jnp.swapaxes(qlat_small[:, :n_samples], 0, 1), jnp.swapaxes(qpe_small[:, :n_samples], 0, 1),
        klat_small[:n_samples, None, :], kpeb_small[:n_samples, None, :],
        cache_ckv, jnp.swapaxes(cache_kpe, 1, 2))
    a_sample = _value_proj(jnp.swapaxes(o_sample, 0, 1), wv)

    ffn1 = ffn_w(1) + ln(1, 1)
    y_main = _ffn(x2_main, *ffn1, proj=(a_main, wob) + ln(1, 0))
    y_sample = _ffn(x2_small[:n_samples], *ffn1, proj=(a_sample, wob) + ln(1, 0))

    pool_prompt = x_prompt[None, :, seq - POOL_HIST:, :]
    pool_sample = jnp.concatenate([state_pool[0][:, 1:], x_sample], axis=1)[None]
    bcast = lambda t: jnp.broadcast_to(t[n_samples:][None], (bsz, N_META, t.shape[1]))
    ckv_prompt = jnp.concatenate([bcast(ckv_small), ckv_main.reshape(bsz, seq, KV_LORA)], axis=1)
    kpe_prompt = jnp.concatenate([bcast(kpe_small), kpe_main.reshape(bsz, seq, QK_ROPE)], axis=1)
    return (y_main.reshape(bsz, seq, D_MODEL), y_sample.reshape(n_samples, 1, D_MODEL),
            pool_prompt, pool_sample, ckv_prompt, kpe_prompt,
            ckv_small[:n_samples, None, :], kpe_small[:n_samples, None, :])
```

```python
import functools

import jax
import jax.numpy as jnp
from jax import lax
from jax.experimental import pallas as pl
from jax.experimental.pallas import tpu as pltpu

F32 = jnp.float32
BF16 = jnp.bfloat16

D_MODEL = 1024
DEPTH = 2
N_META = 16
POOL_WINDOWS = (2, 4, 8, 16)
POOL_GROUP = D_MODEL // len(POOL_WINDOWS)
POOL_HIST = max(POOL_WINDOWS) - 1
N_HEADS = 8
QK_NOPE = 128
QK_ROPE = 64
V_HEAD = 128
KV_LORA = 256
ROPE_THETA = 10000.0
ALPHA = (2 * DEPTH) ** 0.25
LN_EPS = 1e-5
RMS_EPS = 1e-6
SM_SCALE = (QK_NOPE + QK_ROPE) ** -0.5
PAGE_SIZE = 128

LANES = 128
MASK_VALUE = -1e30
VMEM_LIMIT = 48 * 1024 * 1024

ROW_TILE = 1024
FFN_ROW_TILE = 1024
FFN_UNIT_ROWS = 512
FFN_VMEM_LIMIT = 58 * 1024 * 1024
ATTN_TILE = 1024
FLASH_UNIT_ROWS = 256
FLASH_KEY_ROWS = 512
FLASH_MAX_LAG = 1
FLASH_EXP_LAG = 2
FLASH_VALUE_LAG = 3
LOG2_E = 1.4426950408889634
PAGES_PER_CHUNK = 32
DECODE_SLOTS = 4


def _dot(a, b):
    return jnp.dot(a, b, preferred_element_type=F32)


def _dot_nt(a, b):
    return lax.dot_general(a, b, (((1,), (1,)), ((), ())), preferred_element_type=F32)


def _layer_norm(x, g, b):
    mu = jnp.mean(x, axis=-1, keepdims=True)
    xc = x - mu
    var = jnp.mean(xc * xc, axis=-1, keepdims=True)
    return xc * lax.rsqrt(var + LN_EPS) * g + b


def _rms_norm(x, g):
    return x * lax.rsqrt(jnp.mean(x * x, axis=-1, keepdims=True) + RMS_EPS) * g


def _stagger(stage_generators):
    live = dict(enumerate(stage_generators))
    t = 0
    while live:
        for k in sorted(live):
            if t >= k:
                try:
                    next(live[k])
                except StopIteration:
                    del live[k]
        t += 1


def _const_spec(shape):
    zeros = (0,) * len(shape)
    return pl.BlockSpec(shape, lambda *_: zeros, pipeline_mode=pl.Buffered(1))


def _params(n_axes):
    return pltpu.CompilerParams(dimension_semantics=("arbitrary",) * n_axes,
                                vmem_limit_bytes=VMEM_LIMIT)


def _window_sums(xs, n_rows, pad):
    hi = xs.astype(BF16)
    lo = (xs - hi.astype(F32)).astype(BF16)
    k = xs.shape[0]
    diff = (lax.broadcasted_iota(jnp.int32, (n_rows, k), 0) + pad
            - lax.broadcasted_iota(jnp.int32, (n_rows, k), 1))
    sums = []
    for g, win in enumerate(POOL_WINDOWS):
        band = jnp.where(diff >= 0, jnp.where(diff < win, 1.0, 0.0), 0.0).astype(BF16)
        cols = slice(g * POOL_GROUP, (g + 1) * POOL_GROUP)
        sums.append(_dot(band, hi[:, cols]) + _dot(band, lo[:, cols]))
    return sums


def _pool_finish(x, means, w_ref, scale, g, b):
    outs = []
    for gi in range(len(POOL_WINDOWS)):
        cols = slice(gi * POOL_GROUP, (gi + 1) * POOL_GROUP)
        d = means[gi] - x[:, cols]
        outs.append(_dot(d.astype(BF16), w_ref[gi]))
    h = jnp.concatenate(outs, axis=1) * scale
    return _layer_norm(ALPHA * x + h, g, b)


def _blocked_window_means(xs, n_rows):
    hi = xs.astype(BF16)
    lo = (xs - hi.astype(F32)).astype(BF16)
    diff = (lax.broadcasted_iota(jnp.int32, (LANES, 2 * LANES), 0) + LANES
            - lax.broadcasted_iota(jnp.int32, (LANES, 2 * LANES), 1))
    means = []
    for g, win in enumerate(POOL_WINDOWS):
        band = jnp.where(diff >= 0, jnp.where(diff < win, 1.0, 0.0), 0.0).astype(BF16)
        cols = slice(g * POOL_GROUP, (g + 1) * POOL_GROUP)
        blocks = []
        for r in range(0, n_rows, LANES):
            rows = slice(r, r + 2 * LANES)
            blocks.append(_dot(band, hi[rows, cols]) + _dot(band, lo[rows, cols]))
        means.append(jnp.concatenate(blocks, axis=0) * (1.0 / win))
    return means


def _pool_main_rows(x, halo, w_ref, scale, g, b):
    xs = jnp.concatenate([jnp.zeros((LANES - N_META, D_MODEL), F32), halo, x], axis=0)
    return _pool_finish(x, _blocked_window_means(xs, x.shape[0]), w_ref, scale, g, b)


def _pool_meta_kernel(x_ref, w_ref, scale_ref, g_ref, b_ref, o_ref):
    x = x_ref[...]
    n = x.shape[0]
    xs = jnp.concatenate([jnp.zeros((LANES - n, D_MODEL), F32), x], axis=0)
    sums = _window_sums(xs, n, LANES - n)
    row = lax.broadcasted_iota(jnp.int32, (n, 1), 0)
    means = [s / jnp.minimum(row + 1, w).astype(F32) for s, w in zip(sums, POOL_WINDOWS)]
    o_ref[...] = _pool_finish(x, means, w_ref, scale_ref[...], g_ref[...], b_ref[...])


def _pool_sample_kernel(xs_ref, w_ref, scale_ref, g_ref, b_ref, o_ref):
    n_ctx = xs_ref.shape[0]
    x = xs_ref[n_ctx - 1]
    means = []
    for gi, win in enumerate(POOL_WINDOWS):
        cols = slice(gi * POOL_GROUP, (gi + 1) * POOL_GROUP)
        s = xs_ref[n_ctx - win, :, cols]
        for k in range(n_ctx - win + 1, n_ctx):
            s = s + xs_ref[k, :, cols]
        means.append(s * (1.0 / win))
    o_ref[...] = _pool_finish(x, means, w_ref, scale_ref[...], g_ref[...], b_ref[...])


def _pool_weight_specs():
    return [_const_spec((len(POOL_WINDOWS), POOL_GROUP, POOL_GROUP)),
            _const_spec((1, D_MODEL)), _const_spec((1, D_MODEL)), _const_spec((1, D_MODEL))]


def _pool_meta(meta, pool_w, scale, g, b):
    return pl.pallas_call(
        _pool_meta_kernel,
        grid=(1,),
        in_specs=[_const_spec(meta.shape)] + _pool_weight_specs(),
        out_specs=_const_spec(meta.shape),
        out_shape=jax.ShapeDtypeStruct(meta.shape, F32),
        compiler_params=_params(1),
        name="pool_meta",
    )(meta, pool_w, scale, g, b)


def _pool_sample(xs, pool_w, scale, g, b):
    rows = xs.shape[1]
    return pl.pallas_call(
        _pool_sample_kernel,
        grid=(1,),
        in_specs=[_const_spec(xs.shape)] + _pool_weight_specs(),
        out_specs=_const_spec((rows, D_MODEL)),
        out_shape=jax.ShapeDtypeStruct((rows, D_MODEL), F32),
        compiler_params=_params(1),
        name="pool_sample",
    )(xs, pool_w, scale, g, b)


def _ffn_kernel(*refs, mixer, tiles_per_seq, unit_rows):
    if mixer == "proj":
        a_ref, x_ref, wo_ref, g1_ref, b1_ref = refs[:5]
        refs = refs[5:]
    elif mixer == "pool":
        x_ref, prev_ref, meta_ref, pw_ref, scale_ref, g1_ref, b1_ref = refs[:7]
        refs = refs[7:]
    else:
        x_ref = refs[0]
        refs = refs[1:]
    wg_ref, wu_ref, wd_ref, g2_ref, b2_ref, o_ref = refs

    def unit_stages(r0):
        rows = slice(r0, r0 + unit_rows)
        if mixer == "proj":
            x = _layer_norm(ALPHA * x_ref[rows, :] + _dot(a_ref[rows, :], wo_ref[...]), g1_ref[...], b1_ref[...])
        elif mixer == "pool":
            if r0 == 0:
                halo = jnp.where(pl.program_id(0) % tiles_per_seq == 0, meta_ref[...], prev_ref[...])
            else:
                halo = x_ref[r0 - N_META:r0, :]
            x = _pool_main_rows(x_ref[rows, :], halo, pw_ref, scale_ref[...], g1_ref[...], b1_ref[...])
        else:
            x = x_ref[rows, :]
        xb = x.astype(BF16)
        yield
        gate = _dot(xb, wg_ref[...])
        up = _dot(xb, wu_ref[...])
        mid = (gate * (1.0 / (1.0 + jnp.exp(-gate))) * up).astype(BF16)
        ffn = _dot(mid, wd_ref[...])
        yield
        o_ref[rows, :] = _layer_norm(ALPHA * x + ffn, g2_ref[...], b2_ref[...])
        yield

    _stagger([unit_stages(r0) for r0 in range(0, x_ref.shape[0], unit_rows)])


def _ffn(x, wg, wu, wd, layer, g2, b2, proj=None, pool=None):
    rows = x.shape[0]
    layer_spec = lambda w: pl.BlockSpec((None,) + w.shape[1:], lambda *_: (layer, 0, 0),
                                        pipeline_mode=pl.Buffered(1))
    tile = min(FFN_ROW_TILE, rows)
    row_spec = pl.BlockSpec((tile, D_MODEL), lambda i: (i, 0))
    vec_spec = _const_spec((1, D_MODEL))
    args, specs = [], []
    mixer, tiles_per_seq = None, None
    if proj is not None:
        mixer = "proj"
        a, wo, g1, b1 = proj
        args += [a, x, wo, g1, b1]
        specs += [row_spec, row_spec, _const_spec(wo.shape), vec_spec, vec_spec]
    elif pool is not None:
        mixer = "pool"
        meta, pool_w, scale, g1, b1, seq = pool
        assert seq % tile == 0 and tile % LANES == 0
        tiles_per_seq = seq // tile
        halo_blocks = tile // N_META
        args += [x, x, meta, pool_w, scale, g1, b1]
        specs += [row_spec,
                  pl.BlockSpec((N_META, D_MODEL), lambda i: (jnp.maximum(i * halo_blocks - 1, 0), 0)),
                  _const_spec(meta.shape)] + _pool_weight_specs()
    else:
        args += [x]
        specs += [row_spec]
    args += [wg, wu, wd, g2, b2]
    specs += [layer_spec(wg), layer_spec(wu), layer_spec(wd), vec_spec, vec_spec]
    return pl.pallas_call(
        functools.partial(_ffn_kernel, mixer=mixer, tiles_per_seq=tiles_per_seq,
                          unit_rows=FFN_UNIT_ROWS if tile % FFN_UNIT_ROWS == 0 else tile),
        grid=(rows // tile,),
        in_specs=specs,
        out_specs=row_spec,
        out_shape=jax.ShapeDtypeStruct((rows, D_MODEL), F32),
        compiler_params=pltpu.CompilerParams(dimension_semantics=("arbitrary",),
                                             vmem_limit_bytes=FFN_VMEM_LIMIT),
        name="ffn" if mixer is None else "ffn_" + mixer,
    )(*args)


def _latent_query_kernel(x_ref, cos_ref, sin_ref, wkva_ref, kvn_ref, wqa_ref, qn_ref, wqb_ref, wk_ref,
                         ckv_ref, kpe_ref, klat_ref, kpeb_ref, qlat_ref, qpe_ref):
    xb = x_ref[...].astype(BF16)
    cos = cos_ref[...]
    sin = sin_ref[...]
    kv = _dot(xb, wkva_ref[...])
    ckv = _rms_norm(kv[:, :KV_LORA], kvn_ref[...])
    kpe = kv[:, KV_LORA:KV_LORA + QK_ROPE] * cos + kv[:, KV_LORA + QK_ROPE:] * sin
    ckv_ref[...] = ckv
    kpe_ref[...] = kpe
    klat_ref[...] = ckv.astype(BF16)
    kpeb_ref[...] = kpe.astype(BF16)
    qa = _rms_norm(_dot(xb, wqa_ref[...]), qn_ref[...]).astype(BF16)
    q = _dot(qa, wqb_ref[...])
    per_head = QK_NOPE + 2 * QK_ROPE
    for h in range(N_HEADS):
        qh = q[:, h * per_head:(h + 1) * per_head]
        qlat = _dot(qh[:, :QK_NOPE].astype(BF16), wk_ref[h]) * (SM_SCALE * LOG2_E)
        qpe = (qh[:, QK_NOPE:QK_NOPE + QK_ROPE] * cos + qh[:, QK_NOPE + QK_ROPE:] * sin) * (SM_SCALE * LOG2_E)
        qlat_ref[h] = qlat.astype(BF16)
        qpe_ref[h] = qpe.astype(BF16)


def _latent_query(x, cos, sin, wkva, kvn, wqa, qn, wqb, wk):
    rows = x.shape[0]
    tile = min(ROW_TILE, rows)
    pos_tiles = cos.shape[0] // tile
    row = lambda width: pl.BlockSpec((tile, width), lambda i: (i, 0))
    pos = pl.BlockSpec((tile, QK_ROPE), lambda i: (i % pos_tiles, 0))
    head = lambda width: pl.BlockSpec((N_HEADS, tile, width), lambda i: (0, i, 0))
    return pl.pallas_call(
        _latent_query_kernel,
        grid=(rows // tile,),
        in_specs=[row(D_MODEL), pos, pos, _const_spec(wkva.shape), _const_spec(kvn.shape),
                  _const_spec(wqa.shape), _const_spec(qn.shape), _const_spec(wqb.shape),
                  _const_spec(wk.shape)],
        out_specs=[row(KV_LORA), row(QK_ROPE), row(KV_LORA), row(QK_ROPE), head(KV_LORA), head(QK_ROPE)],
        out_shape=[jax.ShapeDtypeStruct((rows, KV_LORA), F32),
                   jax.ShapeDtypeStruct((rows, QK_ROPE), F32),
                   jax.ShapeDtypeStruct((rows, KV_LORA), BF16),
                   jax.ShapeDtypeStruct((rows, QK_ROPE), BF16),
                   jax.ShapeDtypeStruct((N_HEADS, rows, KV_LORA), BF16),
                   jax.ShapeDtypeStruct((N_HEADS, rows, QK_ROPE), BF16)],
        compiler_params=_params(1),
        name="latent_query",
    )(x, cos, sin, wkva, kvn, wqa, qn, wqb, wk)


def _lane_tile(v, width):
    return jnp.concatenate([v] * (width // LANES), axis=1)


def _flash_kernel(ql_ref, qp_ref, kl_ref, kp_ref, ml_ref, mp_ref, wv_ref, o_ref, m_sc, l_sc, acc_sc):
    tq = ql_ref.shape[1]
    i = pl.program_id(1)
    units = [(h, pl.ds(r, FLASH_UNIT_ROWS)) for h in range(N_HEADS) for r in range(0, tq, FLASH_UNIT_ROWS)]

    def scores(u, keys):
        h, rows = units[u]
        return [_dot_nt(ql_ref[h, rows, :], kl) + _dot_nt(qp_ref[h, rows, :], kp) for kl, kp in keys]

    def max_part(u, s_list, first):
        h, rows = units[u]
        m_cur = jnp.max(s_list[0], axis=1, keepdims=True)
        for s in s_list[1:]:
            m_cur = jnp.maximum(m_cur, jnp.max(s, axis=1, keepdims=True))
        if first:
            return jnp.broadcast_to(m_cur, (FLASH_UNIT_ROWS, LANES)), None
        m_prev = m_sc[h, rows, :]
        m_new = jnp.maximum(m_prev, m_cur)
        return m_new, jnp.exp2(m_prev - m_new)

    def exp_part(u, s_list, m_new, alpha, first):
        h, rows = units[u]
        p_list = [jnp.exp2(s - _lane_tile(m_new, s.shape[1])) for s in s_list]
        p_sum = None
        for p in p_list:
            for c in range(p.shape[1] // LANES):
                blk = p[:, c * LANES:(c + 1) * LANES]
                p_sum = blk if p_sum is None else p_sum + blk
        l_sc[h, rows, :] = p_sum if first else alpha * l_sc[h, rows, :] + p_sum
        m_sc[h, rows, :] = m_new
        return [p.astype(BF16) for p in p_list]

    def value_part(u, p_list, alpha, keys, first):
        h, rows = units[u]
        pv = None
        for p, (kl, _) in zip(p_list, keys):
            d = _dot(p, kl)
            pv = d if pv is None else pv + d
        acc_sc[h, rows, :] = pv if first else _lane_tile(alpha, KV_LORA) * acc_sc[h, rows, :] + pv

    def step(unit_keys, first):
        s_q, m_q, p_q = {}, {}, {}
        for t in range(len(units) + FLASH_VALUE_LAG):
            if t < len(units):
                s_q[t] = scores(t, unit_keys(t)[0])
            u = t - FLASH_MAX_LAG
            if 0 <= u < len(units):
                s_q[u] = [s if mk is None else jnp.where(mk(s.shape), s, MASK_VALUE)
                          for s, mk in zip(s_q[u], unit_keys(u)[1])]
                m_q[u] = max_part(u, s_q[u], first)
            u = t - FLASH_EXP_LAG
            if 0 <= u < len(units):
                p_q[u] = exp_part(u, s_q.pop(u), *m_q[u], first)
            u = t - FLASH_VALUE_LAG
            if 0 <= u < len(units):
                value_part(u, p_q.pop(u), m_q.pop(u)[1], unit_keys(u)[0], first)

    def key_tile(start, size):
        off = pl.multiple_of(start, FLASH_UNIT_ROWS)
        return kl_ref[pl.ds(off, size), :], kp_ref[pl.ds(off, size), :]

    meta_mask = lambda shape: lax.broadcasted_iota(jnp.int32, shape, 1) < N_META
    causal_mask = lambda shape: (lax.broadcasted_iota(jnp.int32, shape, 1)
                                 <= lax.broadcasted_iota(jnp.int32, shape, 0))
    meta_keys = (ml_ref[...], mp_ref[...])
    diag_tiles = {r: key_tile(i * tq + r, FLASH_UNIT_ROWS) for r in range(0, tq, FLASH_UNIT_ROWS)}

    def first_keys(u):
        row0 = units[u][1].start
        below = list(range(0, row0, FLASH_UNIT_ROWS))
        return ([meta_keys] + [diag_tiles[r] for r in below] + [diag_tiles[row0]],
                [meta_mask] + [None] * len(below) + [causal_mask])

    step(first_keys, True)

    def body(j, carry):
        full_tile = key_tile(j * FLASH_KEY_ROWS, FLASH_KEY_ROWS)
        step(lambda u: ([full_tile], [None]), False)
        return carry

    lax.fori_loop(0, i * (tq // FLASH_KEY_ROWS), body, 0)

    for h in range(N_HEADS):
        l_row = jnp.sum(l_sc[h], axis=1, keepdims=True)
        o = (acc_sc[h] / l_row).astype(BF16)
        o_ref[:, h * V_HEAD:(h + 1) * V_HEAD] = _dot(o, wv_ref[h]).astype(BF16)


def _flash(qlat, qpe, klat, kpe, mlat, mpe, wv, bsz):
    rows = klat.shape[0]
    seq = rows // bsz
    tq = ATTN_TILE
    assert seq % tq == 0 and tq % FLASH_UNIT_ROWS == 0
    nq = seq // tq
    return pl.pallas_call(
        _flash_kernel,
        grid=(bsz, nq),
        in_specs=[pl.BlockSpec((N_HEADS, tq, KV_LORA), lambda b, i: (0, b * nq + i, 0)),
                  pl.BlockSpec((N_HEADS, tq, QK_ROPE), lambda b, i: (0, b * nq + i, 0)),
                  pl.BlockSpec((seq, KV_LORA), lambda b, i: (b, 0)),
                  pl.BlockSpec((seq, QK_ROPE), lambda b, i: (b, 0)),
                  _const_spec(mlat.shape), _const_spec(mpe.shape), _const_spec(wv.shape)],
        out_specs=pl.BlockSpec((tq, N_HEADS * V_HEAD), lambda b, i: (b * nq + i, 0)),
        out_shape=jax.ShapeDtypeStruct((rows, N_HEADS * V_HEAD), BF16),
        scratch_shapes=[pltpu.VMEM((N_HEADS, tq, LANES), F32), pltpu.VMEM((N_HEADS, tq, LANES), F32),
                        pltpu.VMEM((N_HEADS, tq, KV_LORA), F32)],
        compiler_params=_params(2),
        name="flash_prompt",
    )(qlat, qpe, klat, kpe, mlat, mpe, wv)


def _decode_kernel(pt_ref, ql_ref, qp_ref, kn_ref, pn_ref, ckv_hbm, kpe_hbm, o_ref, kbuf, pbuf, sem,
                   *, n_chunks):
    b = pl.program_id(0)
    n_samples = pl.num_programs(0)
    ahead = DECODE_SLOTS - 1

    def chunk_copies(sample, c):
        slot = (sample * n_chunks + c) % DECODE_SLOTS
        copies = []
        for p in range(PAGES_PER_CHUNK):
            page = pt_ref[sample, c * PAGES_PER_CHUNK + p]
            dst = pl.ds(p * PAGE_SIZE, PAGE_SIZE)
            copies.append(pltpu.make_async_copy(ckv_hbm.at[page], kbuf.at[slot, dst, :], sem.at[0, slot]))
            copies.append(pltpu.make_async_copy(kpe_hbm.at[page], pbuf.at[slot, :, dst], sem.at[1, slot]))
        return slot, copies

    @pl.when(b == 0)
    def _():
        for c in range(ahead):
            for cp in chunk_copies(0, c)[1]:
                cp.start()

    ql = ql_ref[0]
    qp = qp_ref[0]
    kn = kn_ref[0].astype(F32)
    pn = pn_ref[0].astype(F32)
    m_run = (jnp.sum(ql.astype(F32) * kn, axis=1, keepdims=True)
             + jnp.sum(qp.astype(F32) * pn, axis=1, keepdims=True))
    l_run = jnp.ones_like(m_run)
    acc = jnp.broadcast_to(kn, (N_HEADS, KV_LORA))

    for c in range(n_chunks):
        nxt_sample, nxt_c = b + (c + ahead) // n_chunks, (c + ahead) % n_chunks
        if c + ahead < n_chunks:
            for cp in chunk_copies(nxt_sample, nxt_c)[1]:
                cp.start()
        else:
            @pl.when(nxt_sample < n_samples)
            def _():
                for cp in chunk_copies(nxt_sample, nxt_c)[1]:
                    cp.start()
        slot, copies = chunk_copies(b, c)
        for cp in copies:
            cp.wait()
        k = kbuf[slot].astype(BF16)
        pe = pbuf[slot].astype(BF16)
        s = _dot_nt(ql, k) + _dot(qp, pe)
        m_new = jnp.maximum(m_run, jnp.max(s, axis=1, keepdims=True))
        alpha = jnp.exp2(m_run - m_new)
        p = jnp.exp2(s - m_new)
        l_run = alpha * l_run + jnp.sum(p, axis=1, keepdims=True)
        acc = alpha * acc + _dot(p.astype(BF16), k)
        m_run = m_new
    o_ref[0] = acc / l_run


def _decode_attention(page_table, qlat, qpe, knew, pnew, cache_ckv, cache_kpe):
    n_samples, n_pages = page_table.shape
    chunk_rows = PAGES_PER_CHUNK * PAGE_SIZE
    n_chunks = n_pages // PAGES_PER_CHUNK
    assert n_pages % PAGES_PER_CHUNK == 0 and n_chunks >= DECODE_SLOTS - 1
    per_sample = lambda width, lead: pl.BlockSpec((1, lead, width), lambda b, pt: (b, 0, 0))
    grid_spec = pltpu.PrefetchScalarGridSpec(
        num_scalar_prefetch=1,
        grid=(n_samples,),
        in_specs=[per_sample(KV_LORA, N_HEADS), per_sample(QK_ROPE, N_HEADS),
                  per_sample(KV_LORA, 1), per_sample(QK_ROPE, 1),
                  pl.BlockSpec(memory_space=pl.ANY), pl.BlockSpec(memory_space=pl.ANY)],
        out_specs=per_sample(KV_LORA, N_HEADS),
        scratch_shapes=[pltpu.VMEM((DECODE_SLOTS, chunk_rows, KV_LORA), F32),
                        pltpu.VMEM((DECODE_SLOTS, QK_ROPE, chunk_rows), F32),
                        pltpu.SemaphoreType.DMA((2, DECODE_SLOTS))],
    )
    return pl.pallas_call(
        functools.partial(_decode_kernel, n_chunks=n_chunks),
        grid_spec=grid_spec,
        out_shape=jax.ShapeDtypeStruct((n_samples, N_HEADS, KV_LORA), F32),
        compiler_params=_params(1),
        name="decode_attention",
    )(page_table, qlat, qpe, knew, pnew, cache_ckv, cache_kpe)


def _value_proj_kernel(o_ref, wv_ref, a_ref):
    for h in range(N_HEADS):
        a_ref[:, h * V_HEAD:(h + 1) * V_HEAD] = _dot(o_ref[h].astype(BF16), wv_ref[h]).astype(BF16)


def _value_proj(o, wv):
    rows = o.shape[1]
    return pl.pallas_call(
        _value_proj_kernel,
        grid=(1,),
        in_specs=[_const_spec(o.shape), _const_spec(wv.shape)],
        out_specs=_const_spec((rows, N_HEADS * V_HEAD)),
        out_shape=jax.ShapeDtypeStruct((rows, N_HEADS * V_HEAD), BF16),
        compiler_params=_params(1),
        name="value_proj",
    )(o, wv)


def _rope_tables(pos):
    half = QK_ROPE // 2
    inv = ROPE_THETA ** (-jnp.arange(half, dtype=F32) / half)
    ang = pos.astype(F32)[:, None] * inv[None, :]
    cos, sin = jnp.cos(ang), jnp.sin(ang)
    return jnp.concatenate([cos, cos], axis=1), jnp.concatenate([-sin, sin], axis=1)


def _rotate_half_cols(w):
    half = QK_ROPE // 2
    return jnp.concatenate([w[..., half:], w[..., :half]], axis=-1)


def kernel(x_prompt, x_sample, state_pool, cache_ckv, cache_kpe, page_table, meta_tokens,
           ln_g, ln_b, pool_w, pool_scale, w_gate, w_up, w_down,
           wq_a, q_norm, wq_b, wo, wkv_a, kv_norm, wkv_b):
    bsz, seq, _ = x_prompt.shape
    n_samples = x_sample.shape[0]
    past_len = page_table.shape[1] * PAGE_SIZE

    vec = lambda v: v.reshape(1, -1).astype(F32)
    pool_wb = pool_w[0].astype(BF16)
    wg, wu, wd = w_gate.astype(BF16), w_up.astype(BF16), w_down.astype(BF16)
    ffn_w = lambda l: (wg, wu, wd, l)
    wkva = jnp.concatenate([wkv_a, _rotate_half_cols(wkv_a[:, KV_LORA:])], axis=1).astype(BF16)
    wqb3 = wq_b[0].reshape(wq_b.shape[1], N_HEADS, QK_NOPE + QK_ROPE)
    wqb = jnp.concatenate([wqb3, _rotate_half_cols(wqb3[..., QK_NOPE:])], axis=-1)
    wqb = wqb.reshape(wq_b.shape[1], -1).astype(BF16)
    wk = jnp.transpose(wkv_b[:, :, :QK_NOPE], (1, 2, 0)).astype(BF16)
    wv = jnp.transpose(wkv_b[:, :, QK_NOPE:], (1, 0, 2)).astype(BF16)
    wqa, wob = wq_a[0].astype(BF16), wo[0].astype(BF16)
    ln = lambda l, k: (vec(ln_g[l, k]), vec(ln_b[l, k]))

    meta = meta_tokens.astype(F32)
    pool_args = (pool_wb, vec(pool_scale[0])) + ln(0, 0)
    ffn0 = ffn_w(0) + ln(0, 1)
    x2_main = _ffn(x_prompt.reshape(bsz * seq, D_MODEL), *ffn0, pool=(meta,) + pool_args + (seq,))
    x1_meta = _pool_meta(meta, *pool_args)
    xs_sample = jnp.concatenate([jnp.swapaxes(state_pool[0], 0, 1), jnp.swapaxes(x_sample, 0, 1)], axis=0)
    x1_sample = _pool_sample(xs_sample, *pool_args)
    x2_small = _ffn(jnp.concatenate([x1_sample, x1_meta], axis=0), *ffn0)

    cos_main, sin_main = _rope_tables(N_META + jnp.arange(seq))
    pos_small = jnp.concatenate([jnp.full((n_samples,), past_len), jnp.arange(N_META)])
    cos_small, sin_small = _rope_tables(pos_small)
    lq_w = (wkva, vec(kv_norm), wqa, vec(q_norm[0]), wqb, wk)
    ckv_main, kpe_main, klat_main, kpeb_main, qlat_main, qpe_main = _latent_query(
        x2_main, cos_main, sin_main, *lq_w)
    ckv_small, kpe_small, klat_small, kpeb_small, qlat_small, qpe_small = _latent_query(
        x2_small, cos_small, sin_small, *lq_w)

    pad_keys = lambda k: jnp.pad(k[n_samples:], ((0, LANES - N_META), (0, 0)))
    a_main = _flash(qlat_main, qpe_main, klat_main, kpeb_main,
                    pad_keys(klat_small), pad_keys(kpeb_small), wv, bsz)
    o_sample = _decode_attention(
        page_table,
        jnp.swapaxes(qlat_small[:, :n_samples], 0, 1), jnp.swapaxes(qpe_small[:, :n_samples], 0, 1),
        klat_small[:n_samples, None, :], kpeb_small[:n_samples, None, :],
        cache_ckv, jnp.swapaxes(cache_kpe, 1, 2))
    a_sample = _value_proj(jnp.swapaxes(o_sample, 0, 1), wv)

    ffn1 = ffn_w(1) + ln(1, 1)
    y_main = _ffn(x2_main, *ffn1, proj=(a_main, wob) + ln(1, 0))
    y_sample = _ffn(x2_small[:n_samples], *ffn1, proj=(a_sample, wob) + ln(1, 0))

    pool_prompt = x_prompt[None, :, seq - POOL_HIST:, :]
    pool_sample = jnp.concatenate([state_pool[0][:, 1:], x_sample], axis=1)[None]
    bcast = lambda t: jnp.broadcast_to(t[n_samples:][None], (bsz, N_META, t.shape[1]))
    ckv_prompt = jnp.concatenate([bcast(ckv_small), ckv_main.reshape(bsz, seq, KV_LORA)], axis=1)
    kpe_prompt = jnp.concatenate([bcast(kpe_small), kpe_main.reshape(bsz, seq, QK_ROPE)], axis=1)
    return (y_main.reshape(bsz, seq, D_MODEL), y_sample.reshape(n_samples, 1, D_MODEL),
            pool_prompt, pool_sample, ckv_prompt, kpe_prompt,
            ckv_small[:n_samples, None, :], kpe_small[:n_samples, None, :])
```

```python
import functools

import jax
import jax.numpy as jnp
from jax import lax
from jax.experimental import pallas as pl
from jax.experimental.pallas import tpu as pltpu

F32 = jnp.float32
BF16 = jnp.bfloat16

D_MODEL = 1024
DEPTH = 2
N_META = 16
POOL_WINDOWS = (2, 4, 8, 16)
POOL_GROUP = D_MODEL // len(POOL_WINDOWS)
POOL_HIST = max(POOL_WINDOWS) - 1
N_HEADS = 8
QK_NOPE = 128
QK_ROPE = 64
V_HEAD = 128
KV_LORA = 256
ROPE_THETA = 10000.0
ALPHA = (2 * DEPTH) ** 0.25
LN_EPS = 1e-5
RMS_EPS = 1e-6
SM_SCALE = (QK_NOPE + QK_ROPE) ** -0.5
PAGE_SIZE = 128

LANES = 128
MASK_VALUE = -1e30
VMEM_LIMIT = 48 * 1024 * 1024

ROW_TILE = 1024
FFN_ROW_TILE = 1024
FFN_UNIT_ROWS = 512
FFN_VMEM_LIMIT = 58 * 1024 * 1024
ATTN_TILE = 1024
FLASH_UNIT_ROWS = 256
FLASH_KEY_ROWS = 512
FLASH_MAX_LAG = 1
FLASH_EXP_LAG = 2
FLASH_VALUE_LAG = 3
LOG2_E = 1.4426950408889634
PAGES_PER_CHUNK = 32
DECODE_SLOTS = 4


def _dot(a, b):
    return jnp.dot(a, b, preferred_element_type=F32)


def _dot_nt(a, b):
    return lax.dot_general(a, b, (((1,), (1,)), ((), ())), preferred_element_type=F32)


def _layer_norm(x, g, b):
    mu = jnp.mean(x, axis=-1, keepdims=True)
    xc = x - mu
    var = jnp.mean(xc * xc, axis=-1, keepdims=True)
    return xc * lax.rsqrt(var + LN_EPS) * g + b


def _rms_norm(x, g):
    return x * lax.rsqrt(jnp.mean(x * x, axis=-1, keepdims=True) + RMS_EPS) * g


def _stagger(stage_generators):
    live = dict(enumerate(stage_generators))
    t = 0
    while live:
        for k in sorted(live):
            if t >= k:
                try:
                    next(live[k])
                except StopIteration:
                    del live[k]
        t += 1


def _const_spec(shape):
    zeros = (0,) * len(shape)
    return pl.BlockSpec(shape, lambda *_: zeros, pipeline_mode=pl.Buffered(1))


def _params(n_axes):
    return pltpu.CompilerParams(dimension_semantics=("arbitrary",) * n_axes,
                                vmem_limit_bytes=VMEM_LIMIT)


def _window_sums(xs, n_rows, pad):
    hi = xs.astype(BF16)
    lo = (xs - hi.astype(F32)).astype(BF16)
    k = xs.shape[0]
    diff = (lax.broadcasted_iota(jnp.int32, (n_rows, k), 0) + pad
            - lax.broadcasted_iota(jnp.int32, (n_rows, k), 1))
    sums = []
    for g, win in enumerate(POOL_WINDOWS):
        band = jnp.where(diff >= 0, jnp.where(diff < win, 1.0, 0.0), 0.0).astype(BF16)
        cols = slice(g * POOL_GROUP, (g + 1) * POOL_GROUP)
        sums.append(_dot(band, hi[:, cols]) + _dot(band, lo[:, cols]))
    return sums


def _pool_finish(x, means, w_ref, scale, g, b):
    outs = []
    for gi in range(len(POOL_WINDOWS)):
        cols = slice(gi * POOL_GROUP, (gi + 1) * POOL_GROUP)
        d = means[gi] - x[:, cols]
        outs.append(_dot(d.astype(BF16), w_ref[gi]))
    h = jnp.concatenate(outs, axis=1) * scale
    return _layer_norm(ALPHA * x + h, g, b)


def _blocked_window_means(xs, n_rows):
    hi = xs.astype(BF16)
    lo = (xs - hi.astype(F32)).astype(BF16)
    diff = (lax.broadcasted_iota(jnp.int32, (LANES, 2 * LANES), 0) + LANES
            - lax.broadcasted_iota(jnp.int32, (LANES, 2 * LANES), 1))
    means = []
    for g, win in enumerate(POOL_WINDOWS):
        band = jnp.where(diff >= 0, jnp.where(diff < win, 1.0, 0.0), 0.0).astype(BF16)
        cols = slice(g * POOL_GROUP, (g + 1) * POOL_GROUP)
        blocks = []
        for r in range(0, n_rows, LANES):
            rows = slice(r, r + 2 * LANES)
            blocks.append(_dot(band, hi[rows, cols]) + _dot(band, lo[rows, cols]))
        means.append(jnp.concatenate(blocks, axis=0) * (1.0 / win))
    return means


def _pool_main_rows(x, halo, w_ref, scale, g, b):
    xs = jnp.concatenate([jnp.zeros((LANES - N_META, D_MODEL), F32), halo, x], axis=0)
    return _pool_finish(x, _blocked_window_means(xs, x.shape[0]), w_ref, scale, g, b)


def _pool_meta_kernel(x_ref, w_ref, scale_ref, g_ref, b_ref, o_ref):
    x = x_ref[...]
    n = x.shape[0]
    xs = jnp.concatenate([jnp.zeros((LANES - n, D_MODEL), F32), x], axis=0)
    sums = _window_sums(xs, n, LANES - n)
    row = lax.broadcasted_iota(jnp.int32, (n, 1), 0)
    means = [s / jnp.minimum(row + 1, w).astype(F32) for s, w in zip(sums, POOL_WINDOWS)]
    o_ref[...] = _pool_finish(x, means, w_ref, scale_ref[...], g_ref[...], b_ref[...])


def _pool_sample_kernel(xs_ref, w_ref, scale_ref, g_ref, b_ref, o_ref):
    n_ctx = xs_ref.shape[0]
    x = xs_ref[n_ctx - 1]
    means = []
    for gi, win in enumerate(POOL_WINDOWS):
        cols = slice(gi * POOL_GROUP, (gi + 1) * POOL_GROUP)
        s = xs_ref[n_ctx - win, :, cols]
        for k in range(n_ctx - win + 1, n_ctx):
            s = s + xs_ref[k, :, cols]
        means.append(s * (1.0 / win))
    o_ref[...] = _pool_finish(x, means, w_ref, scale_ref[...], g_ref[...], b_ref[...])


def _pool_weight_specs():
    return [_const_spec((len(POOL_WINDOWS), POOL_GROUP, POOL_GROUP)),
            _const_spec((1, D_MODEL)), _const_spec((1, D_MODEL)), _const_spec((1, D_MODEL))]


def _pool_meta(meta, pool_w, scale, g, b):
    return pl.pallas_call(
        _pool_meta_kernel,
        grid=(1,),
        in_specs=[_const_spec(meta.shape)] + _pool_weight_specs(),
        out_specs=_const_spec(meta.shape),
        out_shape=jax.ShapeDtypeStruct(meta.shape, F32),
        compiler_params=_params(1),
        name="pool_meta",
    )(meta, pool_w, scale, g, b)


def _pool_sample(xs, pool_w, scale, g, b):
    rows = xs.shape[1]
    return pl.pallas_call(
        _pool_sample_kernel,
        grid=(1,),
        in_specs=[_const_spec(xs.shape)] + _pool_weight_specs(),
        out_specs=_const_spec((rows, D_MODEL)),
        out_shape=jax.ShapeDtypeStruct((rows, D_MODEL), F32),
        compiler_params=_params(1),
        name="pool_sample",
    )(xs, pool_w, scale, g, b)


def _ffn_kernel(*refs, mixer, tiles_per_seq, unit_rows):
    if mixer == "proj":
        a_ref, x_ref, wo_ref, g1_ref, b1_ref = refs[:5]
        refs = refs[5:]
    elif mixer == "pool":
        x_ref, prev_ref, meta_ref, pw_ref, scale_ref, g1_ref, b1_ref = refs[:7]
        refs = refs[7:]
    else:
        x_ref = refs[0]
        refs = refs[1:]
    wg_ref, wu_ref, wd_ref, g2_ref, b2_ref, o_ref = refs

    def unit_stages(r0):
        rows = slice(r0, r0 + unit_rows)
        if mixer == "proj":
            x = _layer_norm(ALPHA * x_ref[rows, :] + _dot(a_ref[rows, :], wo_ref[...]), g1_ref[...], b1_ref[...])
        elif mixer == "pool":
            if r0 == 0:
                halo = jnp.where(pl.program_id(0) % tiles_per_seq == 0, meta_ref[...], prev_ref[...])
            else:
                halo = x_ref[r0 - N_META:r0, :]
            x = _pool_main_rows(x_ref[rows, :], halo, pw_ref, scale_ref[...], g1_ref[...], b1_ref[...])
        else:
            x = x_ref[rows, :]
        xb = x.astype(BF16)
        yield
        gate = _dot(xb, wg_ref[...])
        up = _dot(xb, wu_ref[...])
        mid = (gate * (1.0 / (1.0 + jnp.exp(-gate))) * up).astype(BF16)
        ffn = _dot(mid, wd_ref[...])
        yield
        o_ref[rows, :] = _layer_norm(ALPHA * x + ffn, g2_ref[...], b2_ref[...])
        yield

    _stagger([unit_stages(r0) for r0 in range(0, x_ref.shape[0], unit_rows)])


def _ffn(x, wg, wu, wd, layer, g2, b2, proj=None, pool=None):
    rows = x.shape[0]
    layer_spec = lambda w: pl.BlockSpec((None,) + w.shape[1:], lambda *_: (layer, 0, 0),
                                        pipeline_mode=pl.Buffered(1))
    tile = min(FFN_ROW_TILE, rows)
    row_spec = pl.BlockSpec((tile, D_MODEL), lambda i: (i, 0))
    vec_spec = _const_spec((1, D_MODEL))
    args, specs = [], []
    mixer, tiles_per_seq = None, None
    if proj is not None:
        mixer = "proj"
        a, wo, g1, b1 = proj
        args += [a, x, wo, g1, b1]
        specs += [row_spec, row_spec, _const_spec(wo.shape), vec_spec, vec_spec]
    elif pool is not None:
        mixer = "pool"
        meta, pool_w, scale, g1, b1, seq = pool
        assert seq % tile == 0 and tile % LANES == 0
        tiles_per_seq = seq // tile
        halo_blocks = tile // N_META
        args += [x, x, meta, pool_w, scale, g1, b1]
        specs += [row_spec,
                  pl.BlockSpec((N_META, D_MODEL), lambda i: (jnp.maximum(i * halo_blocks - 1, 0), 0)),
                  _const_spec(meta.shape)] + _pool_weight_specs()
    else:
        args += [x]
        specs += [row_spec]
    args += [wg, wu, wd, g2, b2]
    specs += [layer_spec(wg), layer_spec(wu), layer_spec(wd), vec_spec, vec_spec]
    return pl.pallas_call(
        functools.partial(_ffn_kernel, mixer=mixer, tiles_per_seq=tiles_per_seq,
                          unit_rows=FFN_UNIT_ROWS if tile % FFN_UNIT_ROWS == 0 else tile),
        grid=(rows // tile,),
        in_specs=specs,
        out_specs=row_spec,
        out_shape=jax.ShapeDtypeStruct((rows, D_MODEL), F32),
        compiler_params=pltpu.CompilerParams(dimension_semantics=("arbitrary",),
                                             vmem_limit_bytes=FFN_VMEM_LIMIT),
        name="ffn" if mixer is None else "ffn_" + mixer,
    )(*args)


def _latent_query_kernel(x_ref, cos_ref, sin_ref, wkva_ref, kvn_ref, wqa_ref, qn_ref, wqb_ref, wk_ref,
                         ckv_ref, kpe_ref, klat_ref, kpeb_ref, qlat_ref, qpe_ref):
    xb = x_ref[...].astype(BF16)
    cos = cos_ref[...]
    sin = sin_ref[...]
    kv = _dot(xb, wkva_ref[...])
    ckv = _rms_norm(kv[:, :KV_LORA], kvn_ref[...])
    kpe = kv[:, KV_LORA:KV_LORA + QK_ROPE] * cos + kv[:, KV_LORA + QK_ROPE:] * sin
    ckv_ref[...] = ckv
    kpe_ref[...] = kpe
    klat_ref[...] = ckv.astype(BF16)
    kpeb_ref[...] = kpe.astype(BF16)
    qa = _rms_norm(_dot(xb, wqa_ref[...]), qn_ref[...]).astype(BF16)
    q = _dot(qa, wqb_ref[...])
    per_head = QK_NOPE + 2 * QK_ROPE
    for h in range(N_HEADS):
        qh = q[:, h * per_head:(h + 1) * per_head]
        qlat = _dot(qh[:, :QK_NOPE].astype(BF16), wk_ref[h]) * (SM_SCALE * LOG2_E)
        qpe = (qh[:, QK_NOPE:QK_NOPE + QK_ROPE] * cos + qh[:, QK_NOPE + QK_ROPE:] * sin) * (SM_SCALE * LOG2_E)
        qlat_ref[h] = qlat.astype(BF16)
        qpe_ref[h] = qpe.astype(BF16)


def _latent_query(x, cos, sin, wkva, kvn, wqa, qn, wqb, wk):
    rows = x.shape[0]
    tile = min(ROW_TILE, rows)
    pos_tiles = cos.shape[0] // tile
    row = lambda width: pl.BlockSpec((tile, width), lambda i: (i, 0))
    pos = pl.BlockSpec((tile, QK_ROPE), lambda i: (i % pos_tiles, 0))
    head = lambda width: pl.BlockSpec((N_HEADS, tile, width), lambda i: (0, i, 0))
    return pl.pallas_call(
        _latent_query_kernel,
        grid=(rows // tile,),
        in_specs=[row(D_MODEL), pos, pos, _const_spec(wkva.shape), _const_spec(kvn.shape),
                  _const_spec(wqa.shape), _const_spec(qn.shape), _const_spec(wqb.shape),
                  _const_spec(wk.shape)],
        out_specs=[row(KV_LORA), row(QK_ROPE), row(KV_LORA), row(QK_ROPE), head(KV_LORA), head(QK_ROPE)],
        out_shape=[jax.ShapeDtypeStruct((rows, KV_LORA), F32),
                   jax.ShapeDtypeStruct((rows, QK_ROPE), F32),
                   jax.ShapeDtypeStruct((rows, KV_LORA), BF16),
                   jax.ShapeDtypeStruct((rows, QK_ROPE), BF16),
                   jax.ShapeDtypeStruct((N_HEADS, rows, KV_LORA), BF16),
                   jax.ShapeDtypeStruct((N_HEADS, rows, QK_ROPE), BF16)],
        compiler_params=_params(1),
        name="latent_query",
    )(x, cos, sin, wkva, kvn, wqa, qn, wqb, wk)


def _lane_tile(v, width):
    return jnp.concatenate([v] * (width // LANES), axis=1)


def _flash_kernel(ql_ref, qp_ref, kl_ref, kp_ref, ml_ref, mp_ref, wv_ref, o_ref, m_sc, l_sc, acc_sc):
    tq = ql_ref.shape[1]
    i = pl.program_id(1)
    units = [(h, pl.ds(r, FLASH_UNIT_ROWS)) for h in range(N_HEADS) for r in range(0, tq, FLASH_UNIT_ROWS)]

    def scores(u, keys):
        h, rows = units[u]
        return [_dot_nt(ql_ref[h, rows, :], kl) + _dot_nt(qp_ref[h, rows, :], kp) for kl, kp in keys]

    def max_part(u, s_list, first):
        h, rows = units[u]
        m_cur = jnp.max(s_list[0], axis=1, keepdims=True)
        for s in s_list[1:]:
            m_cur = jnp.maximum(m_cur, jnp.max(s, axis=1, keepdims=True))
        if first:
            return jnp.broadcast_to(m_cur, (FLASH_UNIT_ROWS, LANES)), None
        m_prev = m_sc[h, rows, :]
        m_new = jnp.maximum(m_prev, m_cur)
        return m_new, jnp.exp2(m_prev - m_new)

    def exp_part(u, s_list, m_new, alpha, first):
        h, rows = units[u]
        p_list = [jnp.exp2(s - _lane_tile(m_new, s.shape[1])) for s in s_list]
        p_sum = None
        for p in p_list:
            for c in range(p.shape[1] // LANES):
                blk = p[:, c * LANES:(c + 1) * LANES]
                p_sum = blk if p_sum is None else p_sum + blk
        l_sc[h, rows, :] = p_sum if first else alpha * l_sc[h, rows, :] + p_sum
        m_sc[h, rows, :] = m_new
        return [p.astype(BF16) for p in p_list]

    def value_part(u, p_list, alpha, keys, first):
        h, rows = units[u]
        pv = None
        for p, (kl, _) in zip(p_list, keys):
            d = _dot(p, kl)
            pv = d if pv is None else pv + d
        acc_sc[h, rows, :] = pv if first else _lane_tile(alpha, KV_LORA) * acc_sc[h, rows, :] + pv

    def step(unit_keys, first):
        s_q, m_q, p_q = {}, {}, {}
        for t in range(len(units) + FLASH_VALUE_LAG):
            if t < len(units):
                s_q[t] = scores(t, unit_keys(t)[0])
            u = t - FLASH_MAX_LAG
            if 0 <= u < len(units):
                s_q[u] = [s if mk is None else jnp.where(mk(s.shape), s, MASK_VALUE)
                          for s, mk in zip(s_q[u], unit_keys(u)[1])]
                m_q[u] = max_part(u, s_q[u], first)
            u = t - FLASH_EXP_LAG
            if 0 <= u < len(units):
                p_q[u] = exp_part(u, s_q.pop(u), *m_q[u], first)
            u = t - FLASH_VALUE_LAG
            if 0 <= u < len(units):
                value_part(u, p_q.pop(u), m_q.pop(u)[1], unit_keys(u)[0], first)

    def key_tile(start, size):
        off = pl.multiple_of(start, FLASH_UNIT_ROWS)
        return kl_ref[pl.ds(off, size), :], kp_ref[pl.ds(off, size), :]

    meta_mask = lambda shape: lax.broadcasted_iota(jnp.int32, shape, 1) < N_META
    causal_mask = lambda shape: (lax.broadcasted_iota(jnp.int32, shape, 1)
                                 <= lax.broadcasted_iota(jnp.int32, shape, 0))
    meta_keys = (ml_ref[...], mp_ref[...])
    diag_tiles = {r: key_tile(i * tq + r, FLASH_UNIT_ROWS) for r in range(0, tq, FLASH_UNIT_ROWS)}

    def first_keys(u):
        row0 = units[u][1].start
        below = list(range(0, row0, FLASH_UNIT_ROWS))
        return ([meta_keys] + [diag_tiles[r] for r in below] + [diag_tiles[row0]],
                [meta_mask] + [None] * len(below) + [causal_mask])

    step(first_keys, True)

    def body(j, carry):
        full_tile = key_tile(j * FLASH_KEY_ROWS, FLASH_KEY_ROWS)
        step(lambda u: ([full_tile], [None]), False)
        return carry

    lax.fori_loop(0, i * (tq // FLASH_KEY_ROWS), body, 0)

    for h in range(N_HEADS):
        l_row = jnp.sum(l_sc[h], axis=1, keepdims=True)
        o = (acc_sc[h] / l_row).astype(BF16)
        o_ref[:, h * V_HEAD:(h + 1) * V_HEAD] = _dot(o, wv_ref[h]).astype(BF16)


def _flash(qlat, qpe, klat, kpe, mlat, mpe, wv, bsz):
    rows = klat.shape[0]
    seq = rows // bsz
    tq = ATTN_TILE
    assert seq % tq == 0 and tq % FLASH_UNIT_ROWS == 0
    nq = seq // tq
    return pl.pallas_call(
        _flash_kernel,
        grid=(bsz, nq),
        in_specs=[pl.BlockSpec((N_HEADS, tq, KV_LORA), lambda b, i: (0, b * nq + i, 0)),
                  pl.BlockSpec((N_HEADS, tq, QK_ROPE), lambda b, i: (0, b * nq + i, 0)),
                  pl.BlockSpec((seq, KV_LORA), lambda b, i: (b, 0)),
                  pl.BlockSpec((seq, QK_ROPE), lambda b, i: (b, 0)),
                  _const_spec(mlat.shape), _const_spec(mpe.shape), _const_spec(wv.shape)],
        out_specs=pl.BlockSpec((tq, N_HEADS * V_HEAD), lambda b, i: (b * nq + i, 0)),
        out_shape=jax.ShapeDtypeStruct((rows, N_HEADS * V_HEAD), BF16),
        scratch_shapes=[pltpu.VMEM((N_HEADS, tq, LANES), F32), pltpu.VMEM((N_HEADS, tq, LANES), F32),
                        pltpu.VMEM((N_HEADS, tq, KV_LORA), F32)],
        compiler_params=_params(2),
        name="flash_prompt",
    )(qlat, qpe, klat, kpe, mlat, mpe, wv)


def _decode_kernel(pt_ref, ql_ref, qp_ref, kn_ref, pn_ref, ckv_hbm, kpe_hbm, o_ref, kbuf, pbuf, sem,
                   *, n_chunks):
    b = pl.program_id(0)
    n_samples = pl.num_programs(0)
    ahead = DECODE_SLOTS - 1

    def chunk_copies(sample, c):
        slot = (sample * n_chunks + c) % DECODE_SLOTS
        copies = []
        for p in range(PAGES_PER_CHUNK):
            page = pt_ref[sample, c * PAGES_PER_CHUNK + p]
            dst = pl.ds(p * PAGE_SIZE, PAGE_SIZE)
            copies.append(pltpu.make_async_copy(ckv_hbm.at[page], kbuf.at[slot, dst, :], sem.at[0, slot]))
            copies.append(pltpu.make_async_copy(kpe_hbm.at[page], pbuf.at[slot, :, dst], sem.at[1, slot]))
        return slot, copies

    def start_chunk(sample, c):
        for n, cp in enumerate(chunk_copies(sample, c)[1]):
            cp.start(priority=n % 2)

    @pl.when(b == 0)
    def _():
        for c in range(ahead):
            start_chunk(0, c)

    ql = ql_ref[0]
    qp = qp_ref[0]
    kn = kn_ref[0].astype(F32)
    pn = pn_ref[0].astype(F32)
    m_run = (jnp.sum(ql.astype(F32) * kn, axis=1, keepdims=True)
             + jnp.sum(qp.astype(F32) * pn, axis=1, keepdims=True))
    l_run = jnp.ones_like(m_run)
    acc = jnp.broadcast_to(kn, (N_HEADS, KV_LORA))

    for c in range(n_chunks):
        nxt_sample, nxt_c = b + (c + ahead) // n_chunks, (c + ahead) % n_chunks
        if c + ahead < n_chunks:
            start_chunk(nxt_sample, nxt_c)
        else:
            @pl.when(nxt_sample < n_samples)
            def _():
                start_chunk(nxt_sample, nxt_c)
        slot, copies = chunk_copies(b, c)
        for cp in copies:
            cp.wait()
        k = kbuf[slot].astype(BF16)
        pe = pbuf[slot].astype(BF16)
        s = _dot_nt(ql, k) + _dot(qp, pe)
        m_new = jnp.maximum(m_run, jnp.max(s, axis=1, keepdims=True))
        alpha = jnp.exp2(m_run - m_new)
        p = jnp.exp2(s - m_new)
        l_run = alpha * l_run + jnp.sum(p, axis=1, keepdims=True)
        acc = alpha * acc + _dot(p.astype(BF16), k)
        m_run = m_new
    o_ref[0] = acc / l_run


def _decode_attention(page_table, qlat, qpe, knew, pnew, cache_ckv, cache_kpe):
    n_samples, n_pages = page_table.shape
    chunk_rows = PAGES_PER_CHUNK * PAGE_SIZE
    n_chunks = n_pages // PAGES_PER_CHUNK
    assert n_pages % PAGES_PER_CHUNK == 0 and n_chunks >= DECODE_SLOTS - 1
    per_sample = lambda width, lead: pl.BlockSpec((1, lead, width), lambda b, pt: (b, 0, 0))
    grid_spec = pltpu.PrefetchScalarGridSpec(
        num_scalar_prefetch=1,
        grid=(n_samples,),
        in_specs=[per_sample(KV_LORA, N_HEADS), per_sample(QK_ROPE, N_HEADS),
                  per_sample(KV_LORA, 1), per_sample(QK_ROPE, 1),
                  pl.BlockSpec(memory_space=pl.ANY), pl.BlockSpec(memory_space=pl.ANY)],
        out_specs=per_sample(KV_LORA, N_HEADS),
        scratch_shapes=[pltpu.VMEM((DECODE_SLOTS, chunk_rows, KV_LORA), F32),
                        pltpu.VMEM((DECODE_SLOTS, QK_ROPE, chunk_rows), F32),
                        pltpu.SemaphoreType.DMA((2, DECODE_SLOTS))],
    )
    return pl.pallas_call(
        functools.partial(_decode_kernel, n_chunks=n_chunks),
        grid_spec=grid_spec,
        out_shape=jax.ShapeDtypeStruct((n_samples, N_HEADS, KV_LORA), F32),
        compiler_params=_params(1),
        name="decode_attention",
    )(page_table, qlat, qpe, knew, pnew, cache_ckv, cache_kpe)


def _value_proj_kernel(o_ref, wv_ref, a_ref):
    for h in range(N_HEADS):
        a_ref[:, h * V_HEAD:(h + 1) * V_HEAD] = _dot(o_ref[h].astype(BF16), wv_ref[h]).astype(BF16)


def _value_proj(o, wv):
    rows = o.shape[1]
    return pl.pallas_call(
        _value_proj_kernel,
        grid=(1,),
        in_specs=[_const_spec(o.shape), _const_spec(wv.shape)],
        out_specs=_const_spec((rows, N_HEADS * V_HEAD)),
        out_shape=jax.ShapeDtypeStruct((rows, N_HEADS * V_HEAD), BF16),
        compiler_params=_params(1),
        name="value_proj",
    )(o, wv)


def _rope_tables(pos):
    half = QK_ROPE // 2
    inv = ROPE_THETA ** (-jnp.arange(half, dtype=F32) / half)
    ang = pos.astype(F32)[:, None] * inv[None, :]
    cos, sin = jnp.cos(ang), jnp.sin(ang)
    return jnp.concatenate([cos, cos], axis=1), jnp.concatenate([-sin, sin], axis=1)


def _rotate_half_cols(w):
    half = QK_ROPE // 2
    return jnp.concatenate([w[..., half:], w[..., :half]], axis=-1)


def kernel(x_prompt, x_sample, state_pool, cache_ckv, cache_kpe, page_table, meta_tokens,
           ln_g, ln_b, pool_w, pool_scale, w_gate, w_up, w_down,
           wq_a, q_norm, wq_b, wo, wkv_a, kv_norm, wkv_b):
    bsz, seq, _ = x_prompt.shape
    n_samples = x_sample.shape[0]
    past_len = page_table.shape[1] * PAGE_SIZE

    vec = lambda v: v.reshape(1, -1).astype(F32)
    pool_wb = pool_w[0].astype(BF16)
    wg, wu, wd = w_gate.astype(BF16), w_up.astype(BF16), w_down.astype(BF16)
    ffn_w = lambda l: (wg, wu, wd, l)
    wkva = jnp.concatenate([wkv_a, _rotate_half_cols(wkv_a[:, KV_LORA:])], axis=1).astype(BF16)
    wqb3 = wq_b[0].reshape(wq_b.shape[1], N_HEADS, QK_NOPE + QK_ROPE)
    wqb = jnp.concatenate([wqb3, _rotate_half_cols(wqb3[..., QK_NOPE:])], axis=-1)
    wqb = wqb.reshape(wq_b.shape[1], -1).astype(BF16)
    wk = jnp.transpose(wkv_b[:, :, :QK_NOPE], (1, 2, 0)).astype(BF16)
    wv = jnp.transpose(wkv_b[:, :, QK_NOPE:], (1, 0, 2)).astype(BF16)
    wqa, wob = wq_a[0].astype(BF16), wo[0].astype(BF16)
    ln = lambda l, k: (vec(ln_g[l, k]), vec(ln_b[l, k]))

    meta = meta_tokens.astype(F32)
    pool_args = (pool_wb, vec(pool_scale[0])) + ln(0, 0)
    ffn0 = ffn_w(0) + ln(0, 1)
    x2_main = _ffn(x_prompt.reshape(bsz * seq, D_MODEL), *ffn0, pool=(meta,) + pool_args + (seq,))
    x1_meta = _pool_meta(meta, *pool_args)
    xs_sample = jnp.concatenate([jnp.swapaxes(state_pool[0], 0, 1), jnp.swapaxes(x_sample, 0, 1)], axis=0)
    x1_sample = _pool_sample(xs_sample, *pool_args)
    x2_small = _ffn(jnp.concatenate([x1_sample, x1_meta], axis=0), *ffn0)

    cos_main, sin_main = _rope_tables(N_META + jnp.arange(seq))
    pos_small = jnp.concatenate([jnp.full((n_samples,), past_len), jnp.arange(N_META)])
    cos_small, sin_small = _rope_tables(pos_small)
    lq_w = (wkva, vec(kv_norm), wqa, vec(q_norm[0]), wqb, wk)
    ckv_main, kpe_main, klat_main, kpeb_main, qlat_main, qpe_main = _latent_query(
        x2_main, cos_main, sin_main, *lq_w)
    ckv_small, kpe_small, klat_small, kpeb_small, qlat_small, qpe_small = _latent_query(
        x2_small, cos_small, sin_small, *lq_w)

    pad_keys = lambda k: jnp.pad(k[n_samples:], ((0, LANES - N_META), (0, 0)))
    a_main = _flash(qlat_main, qpe_main, klat_main, kpeb_main,
                    pad_keys(klat_small), pad_keys(kpeb_small), wv, bsz)
    o_sample = _decode_attention(
        page_table,
        jnp.swapaxes(qlat_small[:, :n_samples], 0, 1), jnp.swapaxes(qpe_small[:, :n_samples], 0, 1),
        klat_small[:n_samples, None, :], kpeb_small[:n_samples, None, :],
        cache_ckv, jnp.swapaxes(cache_kpe, 1, 2))
    a_sample = _value_proj(jnp.swapaxes(o_sample, 0, 1), wv)

    ffn1 = ffn_w(1) + ln(1, 1)
    y_main = _ffn(x2_main, *ffn1, proj=(a_main, wob) + ln(1, 0))
    y_sample = _ffn(x2_small[:n_samples], *ffn1, proj=(a_sample, wob) + ln(1, 0))

    pool_prompt = x_prompt[None, :, seq - POOL_HIST:, :]
    pool_sample = jnp.concatenate([state_pool[0][:, 1:], x_sample], axis=1)[None]
    bcast = lambda t: jnp.broadcast_to(t[n_samples:][None], (bsz, N_META, t.shape[1]))
    ckv_prompt = jnp.concatenate([bcast(ckv_small), ckv_main.reshape(bsz, seq, KV_LORA)], axis=1)
    kpe_prompt = jnp.concatenate([bcast(kpe_small), kpe_main.reshape(bsz, seq, QK_ROPE)], axis=1)
    return (y_main.reshape(bsz, seq, D_MODEL), y_sample.reshape(n_samples, 1, D_MODEL),
            pool_prompt, pool_sample, ckv_prompt, kpe_prompt,
            ckv_small[:n_samples, None, :], kpe_small[:n_samples, None, :])
```
